```python
import jax, jax.numpy as jnp
from jax import lax
import numpy as np

D_MODEL = 1024
BATCH = 4
SEQ = 8192
DEPTH = 1

HG_HEADS = 8
HG_DK = 128
HG_DV = D_MODEL // HG_HEADS
HG_DIM_K = HG_HEADS * HG_DK
HG_DIM_V = HG_HEADS * HG_DV
RET_HEADS = 4
RET_DK = 256
RET_DV = 256
RET_DIM_K = RET_HEADS * RET_DK
RET_DIM_V = RET_HEADS * RET_DV
ROPE_BASE = 10000.0
MAX_POS_OFFSET = 4096
CHUNK = 64
PROJ_WIDTHS = (HG_DIM_K, HG_DIM_K, HG_DIM_K, HG_DIM_V, HG_DIM_V, RET_DIM_K, RET_DIM_K, RET_DIM_V, RET_DIM_V, D_MODEL, D_MODEL)
PROJ_DIM = HG_DIM_K * 3 + HG_DIM_V * 2 + RET_DIM_K * 2 + RET_DIM_V * 2 + D_MODEL * 2
N_GROUPS = 4
EXPERTS_PER_GROUP = 8
N_EXPERTS = N_GROUPS * EXPERTS_PER_GROUP
TOP_K = 2
D_EXPERT = 512
MOE_BLOCK = 128
DEEPNORM_ALPHA = (2 * DEPTH) ** 0.25
DEEPNORM_BETA = (8 * DEPTH) ** -0.25
LN_EPS = 1e-5
RMS_EPS = 1e-6

kernel_name = 'hybrid_hgrn2_retention_hmoe_deepnorm'


def _layer_norm(x, g, b):
    xf = x.astype(jnp.float32)
    mu = jnp.mean(xf, -1, keepdims=True)
    var = jnp.mean(jnp.square(xf - mu), -1, keepdims=True)
    return ((xf - mu) * lax.rsqrt(var + LN_EPS) * g + b).astype(x.dtype)


def _rms_norm_heads(o, g):
    of = o.astype(jnp.float32)
    of = of * lax.rsqrt(jnp.mean(of * of, -1, keepdims=True) + RMS_EPS)
    return of.reshape(*o.shape[:-2], -1) * g


def _to_chunks(t):
    b, l, h, d = t.shape
    return t.reshape(b, l // CHUNK, CHUNK, h, d).transpose(1, 0, 3, 2, 4)


def _from_chunks(t):
    n, b, h, c, d = t.shape
    return t.transpose(1, 0, 3, 2, 4).reshape(b, n * c, h, d)


def _hgrn2_scan(q, k, v, log_f):
    dt = v.dtype
    qc, kc, vc, lfc = (_to_chunks(t.astype(jnp.float32)) for t in (q, k, v, log_f))
    b, h, dk, dv = qc.shape[1], qc.shape[2], qc.shape[-1], vc.shape[-1]
    causal = jnp.tril(jnp.ones((CHUNK, CHUNK), bool))[:, :, None]

    def step(S, inp):
        qb, kb, vb, lf = inp
        G = jnp.cumsum(lf, axis=2)
        g_last = G[:, :, -1:, :]
        decay = jnp.exp(jnp.where(causal, G[:, :, :, None, :] - G[:, :, None, :, :], -jnp.inf))
        attn = jnp.einsum('bhtd,bhsd,bhtsd->bhts', qb, kb, decay)
        o = jnp.einsum('bhts,bhsv->bhtv', attn, vb) + jnp.einsum('bhtd,bhdv->bhtv', qb * jnp.exp(G), S)
        S = jnp.exp(g_last[:, :, 0, :, None]) * S + jnp.einsum('bhsd,bhsv->bhdv', kb * jnp.exp(g_last - G), vb)
        return S, o

    S0 = jnp.zeros((b, h, dk, dv), jnp.float32)
    _, o = lax.scan(step, S0, (qc, kc, vc, lfc))
    return _from_chunks(o).astype(dt)


def _retention_scan(q, k, v, log_gamma):
    dt = v.dtype
    qc, kc, vc = (_to_chunks(t.astype(jnp.float32)) for t in (q, k, v))
    b, h, dk, dv = qc.shape[1], qc.shape[2], qc.shape[-1], vc.shape[-1]
    idx = jnp.arange(CHUNK, dtype=jnp.float32)
    rel = idx[:, None] - idx[None, :]
    lg = log_gamma[:, None]
    decay = jnp.where(rel >= 0, jnp.exp(jnp.maximum(rel, 0.0) * log_gamma[:, None, None]), 0.0)
    cross = jnp.exp((idx + 1.0) * lg)[:, :, None]
    k_dec = jnp.exp((CHUNK - 1.0 - idx) * lg)[:, :, None]
    chunk_dec = jnp.exp(CHUNK * log_gamma)[:, None, None]

    def step(R, inp):
        qb, kb, vb = inp
        inner = jnp.einsum('bhtd,bhsd->bhts', qb, kb) * decay
        o = jnp.einsum('bhts,bhsv->bhtv', inner, vb) + cross * jnp.einsum('bhtd,bhdv->bhtv', qb, R)
        R = chunk_dec * R + jnp.einsum('bhsd,bhsv->bhdv', kb * k_dec, vb)
        return R, o

    R0 = jnp.zeros((b, h, dk, dv), jnp.float32)
    _, o = lax.scan(step, R0, (qc, kc, vc))
    return _from_chunks(o).astype(dt)


def _rotary(t, positions):
    d = t.shape[-1]
    inv = 1.0 / (ROPE_BASE ** jnp.linspace(0.0, 1.0, d // 2, dtype=jnp.float32))
    ang = positions.astype(jnp.float32)[..., None] * inv
    sin, cos = jnp.sin(ang)[:, :, None, :], jnp.cos(ang)[:, :, None, :]
    tf = t.astype(jnp.float32)
    t1, t2 = tf[..., 0::2], tf[..., 1::2]
    out = jnp.stack([t1 * cos - t2 * sin, t1 * sin + t2 * cos], axis=-1).reshape(t.shape)
    return out.astype(t.dtype)


def _token_mixer(x, positions, w_in, lb, hg_norm_g, ret_norm_g, w_branch_hg, w_branch_ret, w_out):
    b, l, _ = x.shape
    proj = x @ w_in
    hq, hf_fwd, hf_bwd, hi, hgate, rq, rk, rv, rgate, ga, gb = jnp.split(proj, np.cumsum(PROJ_WIDTHS)[:-1].tolist(), axis=-1)
    heads = lambda t, n: t.reshape(b, l, n, -1)
    flip = lambda t: jnp.flip(t, axis=1)

    q_h = heads(jax.nn.silu(hq), HG_HEADS)
    v_h = heads(hi, HG_HEADS)

    def hgrn_direction(f_logits, qd, vd):
        f = lb + (1.0 - lb) * jax.nn.sigmoid(f_logits.astype(jnp.float32))
        return _hgrn2_scan(qd, heads(1.0 - f, HG_HEADS), vd, heads(jnp.log(f), HG_HEADS))

    o_hg = hgrn_direction(hf_fwd, q_h, v_h) + flip(hgrn_direction(flip(hf_bwd), flip(q_h), flip(v_h)))
    o_hg = _rms_norm_heads(o_hg, hg_norm_g) * jax.nn.silu(hgate.astype(jnp.float32))
    y_hg = o_hg.astype(x.dtype) @ w_branch_hg

    rq_h = _rotary(heads(rq, RET_HEADS), positions)
    rk_h = _rotary(heads(rk, RET_HEADS), positions) * RET_DK ** -0.5
    rv_h = heads(rv, RET_HEADS)
    log_gamma = jnp.log(1.0 - 2.0 ** (-5.0 - jnp.arange(RET_HEADS, dtype=jnp.float32)))
    o_ret = _retention_scan(rq_h, rk_h, rv_h, log_gamma) + flip(_retention_scan(flip(rq_h), flip(rk_h), flip(rv_h), log_gamma))
    o_ret = _rms_norm_heads(o_ret, ret_norm_g) * jax.nn.silu(rgate.astype(jnp.float32))
    y_ret = o_ret.astype(x.dtype) @ w_branch_ret

    merged = jax.nn.sigmoid(ga) * y_hg + jax.nn.sigmoid(gb) * y_ret
    return merged @ w_out


def _hier_moe(x, w_group, b_group, w_router, b_router, w1, w3, w2):
    t, d = x.shape
    g_logits = (x @ w_group).astype(jnp.float32) + b_group
    grp = jnp.argmax(g_logits, -1)
    p_grp = jnp.take_along_axis(jax.nn.softmax(g_logits, -1), grp[:, None], -1)
    e_logits = ((x @ w_router).astype(jnp.float32) + b_router).reshape(t, N_GROUPS, EXPERTS_PER_GROUP)
    e_logits = jnp.take_along_axis(e_logits, grp[:, None, None], 1)[:, 0]
    top_p, top_i = lax.top_k(jax.nn.softmax(e_logits, -1), TOP_K)
    top_p = top_p / jnp.sum(top_p, -1, keepdims=True)
    wts = (p_grp * top_p).reshape(-1)
    eid = (grp[:, None] * EXPERTS_PER_GROUP + top_i).reshape(-1)
    n_assign = t * TOP_K
    tok = jnp.arange(n_assign, dtype=jnp.int32) // TOP_K

    order = jnp.argsort(eid)
    e_sorted = eid[order]
    counts = jnp.zeros((N_EXPERTS,), jnp.int32).at[eid].add(1)
    start = jnp.cumsum(counts) - counts
    padded = (counts + MOE_BLOCK - 1) // MOE_BLOCK * MOE_BLOCK
    pend = jnp.cumsum(padded)
    pstart = pend - padded
    dest = pstart[e_sorted] + (jnp.arange(n_assign, dtype=jnp.int32) - start[e_sorted])
    n_blocks = -(-n_assign // MOE_BLOCK) + N_EXPERTS
    rows = jnp.full((n_blocks * MOE_BLOCK,), t, jnp.int32).at[dest].set(tok[order])
    row_w = jnp.zeros((n_blocks * MOE_BLOCK,), jnp.float32).at[dest].set(wts[order])
    block_expert = jnp.clip(jnp.searchsorted(pend, jnp.arange(n_blocks, dtype=jnp.int32) * MOE_BLOCK, side='right'), 0, N_EXPERTS - 1)
    x_pad = jnp.concatenate([x, jnp.zeros((1, d), x.dtype)], 0)
    xb = x_pad[rows].reshape(n_blocks, MOE_BLOCK, d)

    def expert_block(args):
        xs, e = args
        return (jax.nn.silu(xs @ w1[e]) * (xs @ w3[e])) @ w2[e]

    yb = lax.map(expert_block, (xb, block_expert))
    y = yb.reshape(-1, d) * row_w[:, None].astype(x.dtype)
    return jax.ops.segment_sum(y, rows, num_segments=t + 1)[:t]


def setup_inputs(seed: int = 0) -> dict:
    key = jax.random.key(seed)
    ks = jax.random.split(key, 20)
    nrm = lambda k, shape, s: jax.random.normal(k, shape, jnp.float32) * s
    beta = DEEPNORM_BETA
    x = nrm(ks[0], (BATCH, SEQ, D_MODEL), 1.0)
    positions = jax.random.randint(ks[1], (BATCH, 1), 0, MAX_POS_OFFSET, jnp.int32) + jnp.arange(SEQ, dtype=jnp.int32)[None, :]
    col_scales = (1.0, 1.0, 1.0, beta, 1.0, 1.0, 1.0, beta, 1.0, 1.0, 1.0)
    col_scale = jnp.concatenate([jnp.full((w,), s, jnp.float32) for w, s in zip(PROJ_WIDTHS, col_scales)])
    w_in = nrm(ks[2], (DEPTH, D_MODEL, PROJ_DIM), D_MODEL ** -0.5) * col_scale
    hg_lb_logits = nrm(ks[3], (DEPTH + 1, HG_DIM_K), 0.5)
    hg_norm_g = 1.0 + nrm(ks[4], (DEPTH, HG_DIM_V), 0.02)
    ret_norm_g = 1.0 + nrm(ks[5], (DEPTH, RET_DIM_V), 0.02)
    w_branch_hg = nrm(ks[6], (DEPTH, HG_DIM_V, D_MODEL), HG_DIM_V ** -0.5 * beta)
    w_branch_ret = nrm(ks[7], (DEPTH, RET_DIM_V, D_MODEL), RET_DIM_V ** -0.5 * beta)
    w_out = nrm(ks[8], (DEPTH, D_MODEL, D_MODEL), D_MODEL ** -0.5 * beta)
    ln1_g = 1.0 + nrm(ks[9], (DEPTH, D_MODEL), 0.02)
    ln1_b = nrm(ks[10], (DEPTH, D_MODEL), 0.02)
    w_group = nrm(ks[11], (DEPTH, D_MODEL, N_GROUPS), D_MODEL ** -0.5)
    b_group = nrm(ks[12], (DEPTH, N_GROUPS), 0.01)
    w_router = nrm(ks[13], (DEPTH, D_MODEL, N_EXPERTS), D_MODEL ** -0.5)
    b_router = nrm(ks[14], (DEPTH, N_EXPERTS), 0.01)
    w1 = nrm(ks[15], (DEPTH, N_EXPERTS, D_MODEL, D_EXPERT), D_MODEL ** -0.5)
    w3 = nrm(ks[16], (DEPTH, N_EXPERTS, D_MODEL, D_EXPERT), D_MODEL ** -0.5)
    w2 = nrm(ks[17], (DEPTH, N_EXPERTS, D_EXPERT, D_MODEL), D_EXPERT ** -0.5 * beta)
    ln2_g = 1.0 + nrm(ks[18], (DEPTH, D_MODEL), 0.02)
    ln2_b = nrm(ks[19], (DEPTH, D_MODEL), 0.02)
    return {'x': x, 'positions': positions, 'w_in': w_in, 'hg_lb_logits': hg_lb_logits,
            'hg_norm_g': hg_norm_g, 'ret_norm_g': ret_norm_g, 'w_branch_hg': w_branch_hg,
            'w_branch_ret': w_branch_ret, 'w_out': w_out, 'ln1_g': ln1_g, 'ln1_b': ln1_b,
            'w_group': w_group, 'b_group': b_group, 'w_router': w_router, 'b_router': b_router,
            'w1': w1, 'w3': w3, 'w2': w2, 'ln2_g': ln2_g, 'ln2_b': ln2_b}


def reference(x, positions, w_in, hg_lb_logits, hg_norm_g, ret_norm_g, w_branch_hg, w_branch_ret,
              w_out, ln1_g, ln1_b, w_group, b_group, w_router, b_router, w1, w3, w2, ln2_g, ln2_b):
    lb_cum = jnp.cumsum(jax.nn.softmax(hg_lb_logits.astype(jnp.float32), axis=0), axis=0)
    for layer in range(DEPTH):
        lb = lb_cum[layer + 1] - lb_cum[0]
        mix = _token_mixer(x, positions, w_in[layer], lb, hg_norm_g[layer], ret_norm_g[layer],
                           w_branch_hg[layer], w_branch_ret[layer], w_out[layer])
        x = _layer_norm(DEEPNORM_ALPHA * x + mix, ln1_g[layer], ln1_b[layer])
        moe = _hier_moe(x.reshape(-1, D_MODEL), w_group[layer], b_group[layer], w_router[layer],
                        b_router[layer], w1[layer], w3[layer], w2[layer]).reshape(x.shape)
        x = _layer_norm(DEEPNORM_ALPHA * x + moe, ln2_g[layer], ln2_b[layer])
    return x
```

```python
import functools

import numpy as np
import jax
import jax.numpy as jnp
from jax import lax
from jax.experimental import pallas as pl
from jax.experimental.pallas import tpu as pltpu

D_MODEL = 1024
HG_HEADS = 8
HG_DK = 128
HG_DV = D_MODEL // HG_HEADS
RET_HEADS = 4
RET_DK = 256
RET_DV = 256
ROPE_BASE = 10000.0
OFF_HQ = 0
OFF_HF_FWD = 1024
OFF_HF_BWD = 2048
OFF_HI = 3072
OFF_HGATE = 4096
OFF_RQ = 5120
OFF_RK = 6144
OFF_RV = 7168
OFF_RGATE = 8192
OFF_GA = 9216
OFF_GB = 10240
PROJ_DIM = 11264
N_GROUPS = 4
EXPERTS_PER_GROUP = 8
N_EXPERTS = N_GROUPS * EXPERTS_PER_GROUP
TOP_K = 2
D_EXPERT = 512
DEPTH = 1
DEEPNORM_ALPHA = (2 * DEPTH) ** 0.25
LN_EPS = 1e-5
RMS_EPS = 1e-6

LANES = 128
CHUNK = 64
SUB = 16
TIME_BLOCK = 512
MOE_BLOCK = 256
ROUTE_TILE = 256
MIX_TILE = 256
VMEM_LIMIT = 56 * 1024 * 1024
NEG_BIG = -1e30

F32 = jnp.float32
BF16 = jnp.bfloat16
_NT = (((1,), (1,)), ((), ()))
_TN = (((0,), (0,)), ((), ()))


def _sigmoid(v):
    return 1.0 / (1.0 + jnp.exp(-v))


def _silu(v):
    return v * _sigmoid(v)


def _dot(a, b, dims=None):
    if dims is None:
        return jnp.dot(a, b, preferred_element_type=F32)
    return lax.dot_general(a, b, dims, preferred_element_type=F32)


def _layer_norm(z, g, b):
    mu = jnp.mean(z, -1, keepdims=True)
    zc = z - mu
    var = jnp.mean(zc * zc, -1, keepdims=True)
    return zc * lax.rsqrt(var + LN_EPS) * g + b


def _rope_kernel(pos_ref, inv_ref, cos_ref, sin_ref):
    ang = pos_ref[...] * inv_ref[...]
    cos_ref[...] = jnp.cos(ang)
    sin_ref[...] = jnp.sin(ang)


def _rope_tables(positions):
    b, l = positions.shape
    tt = min(l, 1024)
    half = RET_DK // 2
    inv = (1.0 / (ROPE_BASE ** jnp.linspace(0.0, 1.0, half, dtype=F32))).reshape(1, half)
    pos = positions.astype(F32).reshape(b, l, 1)
    spec = pl.BlockSpec((None, tt, half), lambda i, j: (i, j, 0))
    return pl.pallas_call(
        _rope_kernel,
        grid=(b, l // tt),
        in_specs=[pl.BlockSpec((None, tt, 1), lambda i, j: (i, j, 0)),
                  pl.BlockSpec((1, half), lambda i, j: (0, 0))],
        out_specs=[spec, spec],
        out_shape=[jax.ShapeDtypeStruct((b, l, half), F32)] * 2,
        compiler_params=pltpu.CompilerParams(dimension_semantics=("parallel", "parallel")),
        name="rope_tables",
    )(pos, inv)


def _matmul_kernel(x_ref, w_ref, o_ref):
    o_ref[...] = _dot(x_ref[...], w_ref[...]).astype(o_ref.dtype)


def _matmul(x, w, out_dtype, tm, tn, name):
    m, k = x.shape
    n = w.shape[1]
    tm, tn = min(tm, m), min(tn, n)
    return pl.pallas_call(
        _matmul_kernel,
        grid=(n // tn, m // tm),
        in_specs=[pl.BlockSpec((tm, k), lambda j, i: (i, 0)),
                  pl.BlockSpec((k, tn), lambda j, i: (0, j))],
        out_specs=pl.BlockSpec((tm, tn), lambda j, i: (i, j)),
        out_shape=jax.ShapeDtypeStruct((m, n), out_dtype),
        compiler_params=pltpu.CompilerParams(dimension_semantics=("parallel", "parallel"),
                                             vmem_limit_bytes=VMEM_LIMIT),
        name=name,
    )(x, w)


def _hgrn_chunk(hq, hf, v, lb, s_t, tri, rev):
    c = hq.shape[0]
    q = _silu(hq)
    f = lb + (1.0 - lb) * _sigmoid(hf)
    k = 1.0 - f
    lf = jnp.log(f)
    g = jnp.dot(tri, lf, precision=lax.Precision.HIGHEST, preferred_element_type=F32)
    g_excl = g - lf
    g_tot = g[0:1] if rev else g[c - 1:c]

    vb = v.astype(BF16)
    o_inter = _dot((q * jnp.exp(g)).astype(BF16), s_t.astype(BF16), _NT)
    k_dec = (k * jnp.exp(g_tot - g)).astype(BF16)
    s_new = jnp.exp(g_tot) * s_t + _dot(vb, k_dec, _TN)

    ones = jnp.ones((LANES, LANES), BF16)
    t_idx = lax.broadcasted_iota(jnp.int32, (SUB, LANES), 0)
    outs = []
    for i in range(c // SUB):
        r0 = i * SUB
        q_i, k_i, g_i, v_i = q[r0:r0 + SUB], k[r0:r0 + SUB], g[r0:r0 + SUB], v[r0:r0 + SUB]
        lo, hi = (r0 + SUB, c) if rev else (0, r0)
        acc = jnp.zeros((SUB, v.shape[1]), F32)
        if hi > lo:
            first = r0 + SUB - 1 if rev else r0
            g_ref = g_excl[first:first + 1]
            q_t = (q_i * jnp.exp(g_i - g_ref)).astype(BF16)
            k_t = (k[lo:hi] * jnp.exp(g_ref - g[lo:hi])).astype(BF16)
            a = _dot(q_t, k_t, _NT)
            acc = _dot(a.astype(BF16), vb[lo:hi])
        prods = []
        for j in range(SUB):
            valid = (t_idx <= j) if rev else (t_idx >= j)
            e = jnp.exp(jnp.where(valid, g_i - g_i[j:j + 1], NEG_BIG))
            prods.append(q_i * k_i[j:j + 1] * e)
        a_rep = _dot(jnp.concatenate(prods, axis=0).astype(BF16), ones)
        for j in range(SUB):
            acc = acc + a_rep[j * SUB:(j + 1) * SUB] * v_i[j:j + 1]
        outs.append(acc)
    return o_inter + jnp.concatenate(outs, axis=0), s_new


def _hgrn_kernel(lb_ref, tri_f_ref, tri_b_ref, qf_ref, ff_ref, vf_ref, qb_ref, fb_ref, vb_ref,
                 o_ref, sf_ref, sb_ref, *, tt, n_chunks):
    n = pl.program_id(2)
    per_block = tt // CHUNK

    @pl.when(n == 0)
    def _():
        sf_ref[...] = jnp.zeros_like(sf_ref)
        sb_ref[...] = jnp.zeros_like(sb_ref)

    lb = lb_ref[...]
    tri_f = tri_f_ref[...]
    tri_b = tri_b_ref[...]

    def put(row0, val, accumulate):
        rows = pl.ds(pl.multiple_of(row0, CHUNK), CHUNK)

        @pl.when(accumulate)
        def _():
            o_ref[rows, :] += val

        @pl.when(jnp.logical_not(accumulate))
        def _():
            o_ref[rows, :] = val

    def body(c, carry):
        it = n * per_block + c
        lo = pl.ds(pl.multiple_of(c * CHUNK, CHUNK), CHUNK)
        o_f, s_f = _hgrn_chunk(qf_ref[lo, :], ff_ref[lo, :], vf_ref[lo, :], lb, sf_ref[...], tri_f, False)
        sf_ref[...] = s_f
        put(it * CHUNK, o_f, 2 * it > n_chunks - 1)
        cb = per_block - 1 - c
        lo_b = pl.ds(pl.multiple_of(cb * CHUNK, CHUNK), CHUNK)
        o_b, s_b = _hgrn_chunk(qb_ref[lo_b, :], fb_ref[lo_b, :], vb_ref[lo_b, :], lb, sb_ref[...], tri_b, True)
        sb_ref[...] = s_b
        put((n_chunks - 1 - it) * CHUNK, o_b, 2 * it >= n_chunks - 1)
        return carry

    lax.fori_loop(0, per_block, body, 0)


def _hgrn2(proj, lb):
    b, l, _ = proj.shape
    tt = min(l, TIME_BLOCK)
    nb = l // tt
    tri_f = jnp.asarray(np.tril(np.ones((CHUNK, CHUNK), np.float32)))
    tri_b = jnp.asarray(np.triu(np.ones((CHUNK, CHUNK), np.float32)))

    def col(off, bwd):
        base = off // LANES
        if bwd:
            return pl.BlockSpec((None, tt, LANES), lambda i, h, n: (i, nb - 1 - n, base + h))
        return pl.BlockSpec((None, tt, LANES), lambda i, h, n: (i, n, base + h))

    tri_spec = pl.BlockSpec((CHUNK, CHUNK), lambda i, h, n: (0, 0))
    return pl.pallas_call(
        functools.partial(_hgrn_kernel, tt=tt, n_chunks=l // CHUNK),
        grid=(b, HG_HEADS, nb),
        in_specs=[pl.BlockSpec((1, LANES), lambda i, h, n: (0, h)), tri_spec, tri_spec,
                  col(OFF_HQ, False), col(OFF_HF_FWD, False), col(OFF_HI, False),
                  col(OFF_HQ, True), col(OFF_HF_BWD, True), col(OFF_HI, True)],
        out_specs=pl.BlockSpec((None, l, HG_DV), lambda i, h, n: (i, 0, h)),
        out_shape=jax.ShapeDtypeStruct((b, l, HG_HEADS * HG_DV), F32),
        scratch_shapes=[pltpu.VMEM((HG_DV, HG_DK), F32), pltpu.VMEM((HG_DV, HG_DK), F32)],
        compiler_params=pltpu.CompilerParams(dimension_semantics=("parallel", "parallel", "arbitrary"),
                                             vmem_limit_bytes=VMEM_LIMIT),
        name="hgrn2",
    )(lb, tri_f, tri_b, proj, proj, proj, proj, proj, proj)


def _rotate(t, cos, sin):
    half = t.shape[1] // 2
    t1, t2 = t[:, :half], t[:, half:]
    return jnp.concatenate([t1 * cos - t2 * sin, t1 * sin + t2 * cos], axis=1)


def _ret_chunk(q, k, v, cos, sin, r, dmat, cross, kdec, cdec):
    qr = _rotate(q, cos, sin).astype(BF16)
    kr = _rotate(k, cos, sin) * (RET_DK ** -0.5)
    vb = v.astype(BF16)
    inner = _dot(qr, kr.astype(BF16), _NT) * dmat
    o = _dot(inner.astype(BF16), vb) + cross * _dot(qr, r.astype(BF16))
    r_new = cdec * r + _dot((kr * kdec).astype(BF16), vb, _TN)
    return o, r_new


def _ret_kernel(dm_f_ref, dm_b_ref, cr_f_ref, cr_b_ref, kd_f_ref, kd_b_ref, cd_ref,
                qf_ref, kf_ref, vf_ref, cf_ref, sf_ref, qb_ref, kb_ref, vb_ref, cb_ref, sb_ref,
                o_ref, rf_ref, rb_ref, *, tt, n_chunks):
    n = pl.program_id(2)
    per_block = tt // CHUNK

    @pl.when(n == 0)
    def _():
        rf_ref[...] = jnp.zeros_like(rf_ref)
        rb_ref[...] = jnp.zeros_like(rb_ref)

    cdec = cd_ref[...]

    def put(row0, val, accumulate):
        rows = pl.ds(pl.multiple_of(row0, CHUNK), CHUNK)

        @pl.when(accumulate)
        def _():
            o_ref[rows, :] += val

        @pl.when(jnp.logical_not(accumulate))
        def _():
            o_ref[rows, :] = val

    def body(c, carry):
        it = n * per_block + c
        lo = pl.ds(pl.multiple_of(c * CHUNK, CHUNK), CHUNK)
        o_f, r_f = _ret_chunk(qf_ref[lo, :], kf_ref[lo, :], vf_ref[lo, :], cf_ref[lo, :], sf_ref[lo, :],
                              rf_ref[...], dm_f_ref[...], cr_f_ref[...], kd_f_ref[...], cdec)
        rf_ref[...] = r_f
        put(it * CHUNK, o_f, 2 * it > n_chunks - 1)
        cb = per_block - 1 - c
        lo_b = pl.ds(pl.multiple_of(cb * CHUNK, CHUNK), CHUNK)
        o_b, r_b = _ret_chunk(qb_ref[lo_b, :], kb_ref[lo_b, :], vb_ref[lo_b, :], cb_ref[lo_b, :], sb_ref[lo_b, :],
                              rb_ref[...], dm_b_ref[...], cr_b_ref[...], kd_b_ref[...], cdec)
        rb_ref[...] = r_b
        put((n_chunks - 1 - it) * CHUNK, o_b, 2 * it >= n_chunks - 1)
        return carry

    lax.fori_loop(0, per_block, body, 0)


def _retention_consts():
    idx = np.arange(CHUNK, dtype=np.float64)
    lg = np.log(1.0 - 2.0 ** (-5.0 - np.arange(RET_HEADS, dtype=np.float64)))[:, None, None]
    rel = idx[:, None] - idx[None, :]
    dm_f = np.where(rel >= 0, np.exp(np.maximum(rel, 0.0) * lg), 0.0)
    dm_b = np.transpose(dm_f, (0, 2, 1))
    wide = np.ones((1, 1, RET_DK))
    cr_f = np.exp((idx + 1.0)[None, :, None] * lg) * wide
    kd_f = np.exp((CHUNK - 1.0 - idx)[None, :, None] * lg) * wide
    cd = np.exp(CHUNK * lg) * wide
    as32 = lambda a: jnp.asarray(a.astype(np.float32))
    return (as32(dm_f), as32(dm_b), as32(cr_f), as32(cr_f[:, ::-1]), as32(kd_f), as32(kd_f[:, ::-1]), as32(cd))


def _retention(proj, cos, sin):
    b, l, _ = proj.shape
    tt = min(l, TIME_BLOCK)
    nb = l // tt
    consts = _retention_consts()

    def tpos(n, bwd):
        return nb - 1 - n if bwd else n

    def col(off, bwd):
        base = off // RET_DK
        return pl.BlockSpec((None, tt, RET_DK), lambda i, h, n: (i, tpos(n, bwd), base + h))

    def tab(bwd):
        return pl.BlockSpec((None, tt, RET_DK // 2), lambda i, h, n: (i, tpos(n, bwd), 0))

    def per_head(shape):
        return pl.BlockSpec((None,) + shape, lambda i, h, n: (h, 0, 0))

    sq, wide, row = (CHUNK, CHUNK), (CHUNK, RET_DK), (1, RET_DK)
    return pl.pallas_call(
        functools.partial(_ret_kernel, tt=tt, n_chunks=l // CHUNK),
        grid=(b, RET_HEADS, nb),
        in_specs=[per_head(sq), per_head(sq), per_head(wide), per_head(wide), per_head(wide), per_head(wide),
                  per_head(row),
                  col(OFF_RQ, False), col(OFF_RK, False), col(OFF_RV, False), tab(False), tab(False),
                  col(OFF_RQ, True), col(OFF_RK, True), col(OFF_RV, True), tab(True), tab(True)],
        out_specs=pl.BlockSpec((None, l, RET_DV), lambda i, h, n: (i, 0, h)),
        out_shape=jax.ShapeDtypeStruct((b, l, RET_HEADS * RET_DV), F32),
        scratch_shapes=[pltpu.VMEM((RET_DK, RET_DV), F32), pltpu.VMEM((RET_DK, RET_DV), F32)],
        compiler_params=pltpu.CompilerParams(dimension_semantics=("parallel", "parallel", "arbitrary"),
                                             vmem_limit_bytes=VMEM_LIMIT),
        name="retention",
    )(*consts, proj, proj, proj, cos, sin, proj, proj, proj, cos, sin)


def _norm_heads(o, n_heads):
    d = o.shape[1] // n_heads
    parts = []
    for h in range(n_heads):
        oh = o[:, h * d:(h + 1) * d]
        parts.append(oh * lax.rsqrt(jnp.mean(oh * oh, -1, keepdims=True) + RMS_EPS))
    return jnp.concatenate(parts, axis=1)


def _mix_kernel(ohg_ref, hgate_ref, oret_ref, rgate_ref, ga_ref, gb_ref, x_ref, ghg_ref, gret_ref,
                wbh_ref, wbr_ref, wo_ref, lng_ref, lnb_ref, wr_ref, x1_ref, logit_ref):
    a = _norm_heads(ohg_ref[...], HG_HEADS) * ghg_ref[...] * _silu(hgate_ref[...])
    y_hg = _dot(a.astype(BF16), wbh_ref[...])
    c = _norm_heads(oret_ref[...], RET_HEADS) * gret_ref[...] * _silu(rgate_ref[...])
    y_ret = _dot(c.astype(BF16), wbr_ref[...])
    merged = _sigmoid(ga_ref[...]) * y_hg + _sigmoid(gb_ref[...]) * y_ret
    mix = _dot(merged.astype(BF16), wo_ref[...])
    x1 = _layer_norm(DEEPNORM_ALPHA * x_ref[...] + mix, lng_ref[...], lnb_ref[...])
    x1_ref[...] = x1
    logit_ref[...] = jnp.dot(x1, wr_ref[...], precision=lax.Precision.HIGHEST, preferred_element_type=F32)


def _mixer_out(o_hg, o_ret, proj, x, g_hg, g_ret, wbh, wbr, wo, ln_g, ln_b, w_route):
    t = x.shape[0]
    tm = min(t, MIX_TILE)
    d = D_MODEL
    rows = lambda cb: pl.BlockSpec((tm, d), lambda i: (i, cb))
    full = lambda shape: pl.BlockSpec(shape, lambda i: (0, 0))
    return pl.pallas_call(
        _mix_kernel,
        grid=(t // tm,),
        in_specs=[rows(0), rows(OFF_HGATE // d), rows(0), rows(OFF_RGATE // d), rows(OFF_GA // d),
                  rows(OFF_GB // d), rows(0), full((1, d)), full((1, d)),
                  full((d, d)), full((d, d)), full((d, d)), full((1, d)), full((1, d)), full((d, LANES))],
        out_specs=[pl.BlockSpec((tm, d), lambda i: (i, 0)), pl.BlockSpec((tm, LANES), lambda i: (i, 0))],
        out_shape=[jax.ShapeDtypeStruct((t, d), F32), jax.ShapeDtypeStruct((t, LANES), F32)],
        compiler_params=pltpu.CompilerParams(dimension_semantics=("parallel",), vmem_limit_bytes=VMEM_LIMIT),
        name="mixer_out",
    )(o_hg, proj, o_ret, proj, proj, proj, x, g_hg, g_ret, wbh, wbr, wo, ln_g, ln_b, w_route)


GROUP_LANE0 = N_EXPERTS


def _route_kernel(logit_ref, bias_ref, tri_ref, idx_ref, wt_ref, cnt_ref, carry_ref):
    i = pl.program_id(0)

    @pl.when(i == 0)
    def _():
        carry_ref[...] = jnp.zeros_like(carry_ref)

    lg = logit_ref[...] + bias_ref[...]
    tm = lg.shape[0]
    lane = lax.broadcasted_iota(jnp.int32, (tm, LANES), 1)
    first_of = lambda hit: jnp.min(jnp.where(hit, lane, LANES), -1, keepdims=True)

    g_mask = (lane >= GROUP_LANE0) & (lane < GROUP_LANE0 + N_GROUPS)
    g_l = jnp.where(g_mask, lg, NEG_BIG)
    g_max = jnp.max(g_l, -1, keepdims=True)
    grp = first_of(g_l == g_max) - GROUP_LANE0
    p_grp = 1.0 / jnp.sum(jnp.where(g_mask, jnp.exp(g_l - g_max), 0.0), -1, keepdims=True)

    e_lo = grp * EXPERTS_PER_GROUP
    e_l = jnp.where((lane >= e_lo) & (lane < e_lo + EXPERTS_PER_GROUP), lg, NEG_BIG)
    m1 = jnp.max(e_l, -1, keepdims=True)
    i1 = first_of(e_l == m1)
    e_l2 = jnp.where(lane == i1, NEG_BIG, e_l)
    m2 = jnp.max(e_l2, -1, keepdims=True)
    i2 = first_of(e_l2 == m2)
    r = jnp.exp(m2 - m1)
    w1 = p_grp / (1.0 + r)
    w2 = p_grp * r / (1.0 + r)

    hit1 = (lane == i1).astype(F32)
    hit2 = (lane == i2).astype(F32)
    both = hit1 + hit2
    before = _dot(tri_ref[...], both.astype(BF16)) + carry_ref[...]
    r1 = jnp.sum(hit1 * before, -1, keepdims=True).astype(jnp.int32)
    r2 = jnp.sum(hit2 * before, -1, keepdims=True).astype(jnp.int32)
    carry_ref[...] += jnp.sum(both, 0, keepdims=True)
    cnt_ref[...] = carry_ref[...]

    slot = lax.broadcasted_iota(jnp.int32, (tm, 8), 1)
    idx_ref[...] = jnp.where(slot == 0, i1, jnp.where(slot == 1, i2, jnp.where(slot == 2, r1, r2)))
    wt_ref[...] = jnp.where(slot == 0, w1, jnp.where(slot == 1, w2, 0.0))


def _route(logits, bias):
    t = logits.shape[0]
    tm = min(t, ROUTE_TILE)
    tri = jnp.asarray(np.tril(np.ones((tm, tm), np.float32), -1)).astype(BF16)
    return pl.pallas_call(
        _route_kernel,
        grid=(t // tm,),
        in_specs=[pl.BlockSpec((tm, LANES), lambda i: (i, 0)), pl.BlockSpec((1, LANES), lambda i: (0, 0)),
                  pl.BlockSpec((tm, tm), lambda i: (0, 0))],
        out_specs=[pl.BlockSpec((tm, 8), lambda i: (i, 0)), pl.BlockSpec((tm, 8), lambda i: (i, 0)),
                   pl.BlockSpec((1, LANES), lambda i: (0, 0))],
        out_shape=[jax.ShapeDtypeStruct((t, 8), jnp.int32), jax.ShapeDtypeStruct((t, 8), F32),
                   jax.ShapeDtypeStruct((1, LANES), F32)],
        scratch_shapes=[pltpu.VMEM((1, LANES), F32)],
        compiler_params=pltpu.CompilerParams(dimension_semantics=("arbitrary",)),
        name="route",
    )(logits, bias, tri)


def _row_copy(src_hbm, row, dst, dst_row, sem):
    return pltpu.make_async_copy(src_hbm.at[pl.ds(row, 1)], dst.at[pl.ds(dst_row, 1)], sem)


def _expert_kernel(be_ref, rows_ref, nused_ref, x_hbm, w1_ref, w3_ref, w2_ref, y_ref, buf, sem):
    i = pl.program_id(0)
    n_used = nused_ref[0]
    blk = buf.shape[1]

    def start(block, slot):
        def body(r, carry):
            _row_copy(x_hbm, rows_ref[block * blk + r], buf.at[slot], r, sem.at[slot]).start()
            return carry
        lax.fori_loop(0, blk, body, 0)

    def wait(slot):
        def body(r, carry):
            _row_copy(x_hbm, 0, buf.at[slot], r, sem.at[slot]).wait()
            return carry
        lax.fori_loop(0, blk, body, 0)

    @pl.when(i == 0)
    def _():
        start(0, 0)

    @pl.when(i + 1 < n_used)
    def _():
        start(i + 1, (i + 1) % 2)

    @pl.when(i < n_used)
    def _():
        slot = i % 2
        wait(slot)
        xs = buf[slot].astype(BF16)
        h = _silu(_dot(xs, w1_ref[...])) * _dot(xs, w3_ref[...])
        y_ref[...] = _dot(h.astype(BF16), w2_ref[...])

    @pl.when(i >= n_used)
    def _():
        y_ref[...] = jnp.zeros_like(y_ref)


def _experts(x1, block_expert, rows, n_used, w1, w3, w2, blk):
    n_blocks = block_expert.shape[0]
    d, de = D_MODEL, D_EXPERT
    grid_spec = pltpu.PrefetchScalarGridSpec(
        num_scalar_prefetch=3,
        grid=(n_blocks,),
        in_specs=[pl.BlockSpec(memory_space=pl.ANY),
                  pl.BlockSpec((None, d, de), lambda i, be, rw, nu: (be[i], 0, 0)),
                  pl.BlockSpec((None, d, de), lambda i, be, rw, nu: (be[i], 0, 0)),
                  pl.BlockSpec((None, de, d), lambda i, be, rw, nu: (be[i], 0, 0))],
        out_specs=pl.BlockSpec((blk, d), lambda i, be, rw, nu: (i, 0)),
        scratch_shapes=[pltpu.VMEM((2, blk, d), F32), pltpu.SemaphoreType.DMA((2,))],
    )
    return pl.pallas_call(
        _expert_kernel,
        grid_spec=grid_spec,
        out_shape=jax.ShapeDtypeStruct((n_blocks * blk, d), F32),
        compiler_params=pltpu.CompilerParams(dimension_semantics=("arbitrary",), vmem_limit_bytes=VMEM_LIMIT),
        name="experts",
    )(block_expert, rows, n_used, x1, w1, w3, w2)


def _combine_kernel(dest_ref, y_hbm, x1_ref, wt_ref, lng_ref, lnb_ref, o_ref, buf, sem):
    i = pl.program_id(0)
    n = pl.num_programs(0)
    tm = x1_ref.shape[0]

    def start(tile, slot):
        def body(r, carry):
            for k in range(TOP_K):
                _row_copy(y_hbm, dest_ref[(tile * tm + r) * TOP_K + k], buf.at[slot, k], r, sem.at[slot]).start()
            return carry
        lax.fori_loop(0, tm, body, 0)

    def wait(slot):
        def body(r, carry):
            for k in range(TOP_K):
                _row_copy(y_hbm, 0, buf.at[slot, k], r, sem.at[slot]).wait()
            return carry
        lax.fori_loop(0, tm, body, 0)

    @pl.when(i == 0)
    def _():
        start(0, 0)

    @pl.when(i + 1 < n)
    def _():
        start(i + 1, (i + 1) % 2)

    slot = i % 2
    wait(slot)
    wt = wt_ref[...]
    moe = wt[:, 0:1] * buf[slot, 0] + wt[:, 1:2] * buf[slot, 1]
    o_ref[...] = _layer_norm(DEEPNORM_ALPHA * x1_ref[...] + moe, lng_ref[...], lnb_ref[...])


def _combine(dest, y, x1, wts, ln_g, ln_b):
    t, d = x1.shape
    tm = min(t, ROUTE_TILE)
    grid_spec = pltpu.PrefetchScalarGridSpec(
        num_scalar_prefetch=1,
        grid=(t // tm,),
        in_specs=[pl.BlockSpec(memory_space=pl.ANY),
                  pl.BlockSpec((tm, d), lambda i, ds: (i, 0)),
                  pl.BlockSpec((tm, 8), lambda i, ds: (i, 0)),
                  pl.BlockSpec((1, d), lambda i, ds: (0, 0)),
                  pl.BlockSpec((1, d), lambda i, ds: (0, 0))],
        out_specs=pl.BlockSpec((tm, d), lambda i, ds: (i, 0)),
        scratch_shapes=[pltpu.VMEM((2, TOP_K, tm, d), F32), pltpu.SemaphoreType.DMA((2,))],
    )
    return pl.pallas_call(
        _combine_kernel,
        grid_spec=grid_spec,
        out_shape=jax.ShapeDtypeStruct((t, d), F32),
        compiler_params=pltpu.CompilerParams(dimension_semantics=("arbitrary",), vmem_limit_bytes=VMEM_LIMIT),
        name="combine",
    )(dest, y, x1, wts, ln_g, ln_b)


def _deinterleave_cols(w):
    dk = RET_DK
    wh = w.reshape(w.shape[0], RET_HEADS, dk // 2, 2)
    return jnp.concatenate([wh[..., 0], wh[..., 1]], axis=-1).reshape(w.shape[0], RET_HEADS * dk)


def kernel(x, positions, w_in, hg_lb_logits, hg_norm_g, ret_norm_g, w_branch_hg, w_branch_ret, w_out,
           ln1_g, ln1_b, w_group, b_group, w_router, b_router, w1, w3, w2, ln2_g, ln2_b):
    b, l, d = x.shape
    t = b * l
    lb_cum = jnp.cumsum(jax.nn.softmax(hg_lb_logits.astype(F32), axis=0), axis=0)
    cos, sin = _rope_tables(positions)
    xf = x.reshape(t, d)
    for layer in range(DEPTH):
        lb = (lb_cum[layer + 1] - lb_cum[0]).reshape(1, -1)
        w = w_in[layer]
        w = jnp.concatenate([w[:, :OFF_RQ], _deinterleave_cols(w[:, OFF_RQ:OFF_RK]),
                             _deinterleave_cols(w[:, OFF_RK:OFF_RV]), w[:, OFF_RV:]], axis=1)
        proj = _matmul(xf.astype(BF16), w.astype(BF16), F32, 1024, 1024, "in_proj")
        proj3 = proj.reshape(b, l, PROJ_DIM)
        o_hg = _hgrn2(proj3, lb).reshape(t, d)
        o_ret = _retention(proj3, cos, sin).reshape(t, d)

        w_route = jnp.zeros((d, LANES), F32)
        w_route = w_route.at[:, :N_EXPERTS].set(w_router[layer]).at[:, GROUP_LANE0:GROUP_LANE0 + N_GROUPS].set(w_group[layer])
        b_route = jnp.zeros((1, LANES), F32)
        b_route = b_route.at[0, :N_EXPERTS].set(b_router[layer]).at[0, GROUP_LANE0:GROUP_LANE0 + N_GROUPS].set(b_group[layer])
        row = lambda v: v.reshape(1, -1)
        x1, logits = _mixer_out(o_hg, o_ret, proj, xf, row(hg_norm_g[layer]), row(ret_norm_g[layer]),
                                w_branch_hg[layer].astype(BF16), w_branch_ret[layer].astype(BF16),
                                w_out[layer].astype(BF16), row(ln1_g[layer]), row(ln1_b[layer]), w_route)

        idx, wts, counts = _route(logits, b_route)
        blk = MOE_BLOCK
        counts = counts[0, :N_EXPERTS].astype(jnp.int32)
        padded = (counts + blk - 1) // blk * blk
        pend = jnp.cumsum(padded)
        pstart = pend - padded
        dest = pstart[idx[:, :TOP_K]] + idx[:, TOP_K:2 * TOP_K]
        n_blocks = t * TOP_K // blk + N_EXPERTS
        tok = jnp.broadcast_to(jnp.arange(t, dtype=jnp.int32)[:, None], (t, TOP_K))
        rows = jnp.zeros((n_blocks * blk,), jnp.int32).at[dest.reshape(-1)].set(tok.reshape(-1))
        block_row0 = jnp.arange(n_blocks, dtype=jnp.int32) * blk
        block_expert = jnp.minimum(jnp.sum((pend[None, :] <= block_row0[:, None]).astype(jnp.int32), axis=1),
                                   N_EXPERTS - 1)
        n_used = (pend[-1:] // blk).astype(jnp.int32)
        y = _experts(x1, block_expert, rows, n_used, w1[layer].astype(BF16), w3[layer].astype(BF16),
                     w2[layer].astype(BF16), blk)
        xf = _combine(dest.reshape(-1).astype(jnp.int32), y, x1, wts, row(ln2_g[layer]), row(ln2_b[layer]))
    return xf.reshape(b, l, d)
```

```python
import functools

import numpy as np
import jax
import jax.numpy as jnp
from jax import lax
from jax.experimental import pallas as pl
from jax.experimental.pallas import tpu as pltpu

D_MODEL = 1024
HG_HEADS = 8
HG_DK = 128
HG_DV = D_MODEL // HG_HEADS
RET_HEADS = 4
RET_DK = 256
RET_DV = 256
ROPE_BASE = 10000.0
OFF_HQ = 0
OFF_HF_FWD = 1024
OFF_HF_BWD = 2048
OFF_HI = 3072
OFF_HGATE = 4096
OFF_RQ = 5120
OFF_RK = 6144
OFF_RV = 7168
OFF_RGATE = 8192
OFF_GA = 9216
OFF_GB = 10240
PROJ_DIM = 11264
N_GROUPS = 4
EXPERTS_PER_GROUP = 8
N_EXPERTS = N_GROUPS * EXPERTS_PER_GROUP
TOP_K = 2
D_EXPERT = 512
DEPTH = 1
DEEPNORM_ALPHA = (2 * DEPTH) ** 0.25
LN_EPS = 1e-5
RMS_EPS = 1e-6

LANES = 128
CHUNK = 64
RET_CHUNK = 128
RET_SPAN = 2
SUB = 16
TIME_BLOCK = 512
MOE_BLOCK = 256
ROUTE_TILE = 256
MIX_TILE = 256
VMEM_LIMIT = 56 * 1024 * 1024
NEG_BIG = -1e30
FAST_MIN_GATE = float(np.exp(-60.0 / SUB))
GROW_CAP = 64.0
FAST_SPAN = 4

F32 = jnp.float32
BF16 = jnp.bfloat16
_NT = (((1,), (1,)), ((), ()))
_TN = (((0,), (0,)), ((), ()))


def _sigmoid(v):
    return 1.0 / (1.0 + jnp.exp(-v))


def _silu(v):
    return v * _sigmoid(v)


def _dot(a, b, dims=None):
    if dims is None:
        return jnp.dot(a, b, preferred_element_type=F32)
    return lax.dot_general(a, b, dims, preferred_element_type=F32)


def _layer_norm(z, g, b):
    mu = jnp.mean(z, -1, keepdims=True)
    zc = z - mu
    var = jnp.mean(zc * zc, -1, keepdims=True)
    return zc * lax.rsqrt(var + LN_EPS) * g + b


def _rope_kernel(pos_ref, inv_ref, cos_ref, sin_ref):
    ang = pos_ref[...] * inv_ref[...]
    cos_ref[...] = jnp.cos(ang)
    sin_ref[...] = jnp.sin(ang)


def _rope_tables(positions):
    b, l = positions.shape
    tt = min(l, 1024)
    half = RET_DK // 2
    inv = (1.0 / (ROPE_BASE ** jnp.linspace(0.0, 1.0, half, dtype=F32))).reshape(1, half)
    pos = positions.astype(F32).reshape(b, l, 1)
    spec = pl.BlockSpec((None, tt, half), lambda i, j: (i, j, 0))
    return pl.pallas_call(
        _rope_kernel,
        grid=(b, l // tt),
        in_specs=[pl.BlockSpec((None, tt, 1), lambda i, j: (i, j, 0)),
                  pl.BlockSpec((1, half), lambda i, j: (0, 0))],
        out_specs=[spec, spec],
        out_shape=[jax.ShapeDtypeStruct((b, l, half), F32)] * 2,
        compiler_params=pltpu.CompilerParams(dimension_semantics=("parallel", "parallel")),
        name="rope_tables",
    )(pos, inv)


def _matmul_kernel(x_ref, w_ref, o_ref):
    o_ref[...] = _dot(x_ref[...], w_ref[...]).astype(o_ref.dtype)


def _matmul(x, w, out_dtype, tm, tn, name):
    m, k = x.shape
    n = w.shape[1]
    tm, tn = min(tm, m), min(tn, n)
    return pl.pallas_call(
        _matmul_kernel,
        grid=(n // tn, m // tm),
        in_specs=[pl.BlockSpec((tm, k), lambda j, i: (i, 0)),
                  pl.BlockSpec((k, tn), lambda j, i: (0, j))],
        out_specs=pl.BlockSpec((tm, tn), lambda j, i: (i, j)),
        out_shape=jax.ShapeDtypeStruct((m, n), out_dtype),
        compiler_params=pltpu.CompilerParams(dimension_semantics=("parallel", "parallel"),
                                             vmem_limit_bytes=VMEM_LIMIT),
        name=name,
    )(x, w)


def _hgrn_chunk(hq, hf, v, lb, s_t, tri, rev):
    c = hq.shape[0]
    q = _silu(hq)
    f = lb + (1.0 - lb) * _sigmoid(hf)
    k = 1.0 - f
    lf = jnp.log(f)
    g = jnp.dot(tri, lf, precision=lax.Precision.HIGHEST, preferred_element_type=F32)
    g_excl = g - lf
    g_tot = g[0:1] if rev else g[c - 1:c]

    vb = v.astype(BF16)
    o_inter = _dot((q * jnp.exp(g)).astype(BF16), s_t.astype(BF16), _NT)
    k_dec = (k * jnp.exp(g_tot - g)).astype(BF16)
    s_new = jnp.exp(g_tot) * s_t + _dot(vb, k_dec, _TN)

    ones = jnp.ones((LANES, LANES), BF16)
    t_idx = lax.broadcasted_iota(jnp.int32, (SUB, LANES), 0)
    outs = []
    for i in range(c // SUB):
        r0 = i * SUB
        q_i, k_i, g_i, v_i = q[r0:r0 + SUB], k[r0:r0 + SUB], g[r0:r0 + SUB], v[r0:r0 + SUB]
        lo, hi = (r0 + SUB, c) if rev else (0, r0)
        acc = jnp.zeros((SUB, v.shape[1]), F32)
        if hi > lo:
            first = r0 + SUB - 1 if rev else r0
            g_ref = g_excl[first:first + 1]
            q_t = (q_i * jnp.exp(g_i - g_ref)).astype(BF16)
            k_t = (k[lo:hi] * jnp.exp(g_ref - g[lo:hi])).astype(BF16)
            a = _dot(q_t, k_t, _NT)
            acc = _dot(a.astype(BF16), vb[lo:hi])
        prods = []
        for j in range(SUB):
            valid = (t_idx <= j) if rev else (t_idx >= j)
            e = jnp.exp(jnp.where(valid, g_i - g_i[j:j + 1], NEG_BIG))
            prods.append(q_i * k_i[j:j + 1] * e)
        a_rep = _dot(jnp.concatenate(prods, axis=0).astype(BF16), ones)
        for j in range(SUB):
            acc = acc + a_rep[j * SUB:(j + 1) * SUB] * v_i[j:j + 1]
        outs.append(acc)
    return o_inter + jnp.concatenate(outs, axis=0), s_new


def _hgrn_fast_span(lb, tri, allowed, q_ref, f_ref, v_ref, s_ref, chunk_ids, rev):
    c = CHUNK
    tri_b16 = tri.astype(BF16)
    prepped = []
    for cid in chunk_ids:
        rows = pl.ds(pl.multiple_of(cid * c, c), c)
        q = _silu(q_ref[rows, :])
        f = lb + (1.0 - lb) * _sigmoid(f_ref[rows, :])
        lf = jnp.log(f)
        hi = lf.astype(BF16)
        lo = (lf - hi.astype(F32)).astype(BF16)
        g2 = _dot(tri_b16, jnp.concatenate([hi, lo], axis=1))
        g = g2[:, :LANES] + g2[:, LANES:]
        prepped.append((q, 1.0 - f, lf, g, v_ref[rows, :].astype(BF16)))

    staged = []
    for q, k, lf, g, vb in prepped:
        eg = jnp.exp(g)
        total = eg[0:1] if rev else eg[c - 1:c]
        g_tot = g[0:1] if rev else g[c - 1:c]
        q_inter = (q * eg).astype(BF16)
        d_s = _dot(vb, (k * jnp.exp(g_tot - g)).astype(BF16), _TN)
        g_excl = g - lf
        scores = []
        for i in range(c // SUB):
            r0 = i * SUB
            first = r0 + SUB - 1 if rev else r0
            g_ref = g_excl[first:first + 1]
            q_t = (q[r0:r0 + SUB] * jnp.exp(g[r0:r0 + SUB] - g_ref)).astype(BF16)
            k_t = (k * jnp.exp(jnp.minimum(g_ref - g, GROW_CAP))).astype(BF16)
            scores.append(_dot(q_t, k_t, _NT))
        staged.append((q_inter, d_s, total, jnp.concatenate(scores, axis=0), vb))

    intra = [_dot((sc * allowed).astype(BF16), vb) for _, _, _, sc, vb in staged]

    outs = []
    s_t = s_ref[...]
    for (q_inter, d_s, total, _, _), o_intra in zip(staged, intra):
        outs.append(_dot(q_inter, s_t.astype(BF16), _NT) + o_intra)
        s_t = total * s_t + d_s
    s_ref[...] = s_t
    return outs


def _hgrn_kernel(lb_ref, tri_f_ref, tri_b_ref, qf_ref, ff_ref, vf_ref, qb_ref, fb_ref, vb_ref,
                 o_ref, sf_ref, sb_ref, *, tt):
    n = pl.program_id(2)
    per_block = tt // CHUNK
    n_chunks = pl.num_programs(2) * per_block

    @pl.when(n == 0)
    def _():
        sf_ref[...] = jnp.zeros_like(sf_ref)
        sb_ref[...] = jnp.zeros_like(sb_ref)
        o_ref[...] = jnp.zeros_like(o_ref)

    lb = lb_ref[...]

    def smallest_gate(f_ref):
        return jnp.min(lb + (1.0 - lb) * _sigmoid(jnp.min(f_ref[...], axis=0, keepdims=True)))

    fast_ok = jnp.minimum(smallest_gate(ff_ref), smallest_gate(fb_ref)) > FAST_MIN_GATE

    def out_rows(chunk):
        return pl.ds(pl.multiple_of(chunk * CHUNK, CHUNK), CHUNK)

    @pl.when(fast_ok)
    def _():
        span = min(FAST_SPAN, per_block)

        def body(r, carry):
            ids = [r * span + j for j in range(span)]
            o_f = _hgrn_fast_span(lb, tri_f_ref[...], tri_f_ref[...], qf_ref, ff_ref, vf_ref, sf_ref, ids, False)
            o_b = _hgrn_fast_span(lb, tri_b_ref[...], tri_b_ref[...], qb_ref, fb_ref, vb_ref, sb_ref,
                                  [per_block - 1 - i for i in ids], True)
            for j in range(span):
                it = n * per_block + ids[j]
                o_ref[out_rows(it), :] += o_f[j]
                o_ref[out_rows(n_chunks - 1 - it), :] += o_b[j]
            return carry

        lax.fori_loop(0, per_block // span, body, 0)

    @pl.when(jnp.logical_not(fast_ok))
    def _():
        def body(c, carry):
            it = n * per_block + c
            o_f, s_f = _hgrn_chunk(qf_ref[out_rows(c), :], ff_ref[out_rows(c), :], vf_ref[out_rows(c), :], lb,
                                   sf_ref[...], tri_f_ref[...], False)
            sf_ref[...] = s_f
            o_ref[out_rows(it), :] += o_f
            cb = per_block - 1 - c
            o_b, s_b = _hgrn_chunk(qb_ref[out_rows(cb), :], fb_ref[out_rows(cb), :], vb_ref[out_rows(cb), :], lb,
                                   sb_ref[...], tri_b_ref[...], True)
            sb_ref[...] = s_b
            o_ref[out_rows(n_chunks - 1 - it), :] += o_b
            return carry

        lax.fori_loop(0, per_block, body, 0)


def _hgrn2(proj, lb):
    b, l, _ = proj.shape
    tt = min(l, TIME_BLOCK)
    nb = l // tt
    ones = np.ones((CHUNK, CHUNK), np.float32)
    tri_f, tri_b = jnp.asarray(np.tril(ones)), jnp.asarray(np.triu(ones))

    def col(off, bwd):
        base = off // LANES
        if bwd:
            return pl.BlockSpec((None, tt, LANES), lambda i, h, n: (i, nb - 1 - n, base + h))
        return pl.BlockSpec((None, tt, LANES), lambda i, h, n: (i, n, base + h))

    sq_spec = pl.BlockSpec((CHUNK, CHUNK), lambda i, h, n: (0, 0))
    return pl.pallas_call(
        functools.partial(_hgrn_kernel, tt=tt),
        grid=(b, HG_HEADS, nb),
        in_specs=[pl.BlockSpec((1, LANES), lambda i, h, n: (0, h)), sq_spec, sq_spec,
                  col(OFF_HQ, False), col(OFF_HF_FWD, False), col(OFF_HI, False),
                  col(OFF_HQ, True), col(OFF_HF_BWD, True), col(OFF_HI, True)],
        out_specs=pl.BlockSpec((None, l, HG_DV), lambda i, h, n: (i, 0, h)),
        out_shape=jax.ShapeDtypeStruct((b, l, HG_HEADS * HG_DV), F32),
        scratch_shapes=[pltpu.VMEM((HG_DV, HG_DK), F32), pltpu.VMEM((HG_DV, HG_DK), F32)],
        compiler_params=pltpu.CompilerParams(dimension_semantics=("parallel", "parallel", "arbitrary"),
                                             vmem_limit_bytes=VMEM_LIMIT),
        name="hgrn2",
    )(lb, tri_f, tri_b, proj, proj, proj, proj, proj, proj)


def _rotate(t, cos, sin):
    half = t.shape[1] // 2
    t1, t2 = t[:, :half], t[:, half:]
    return jnp.concatenate([t1 * cos - t2 * sin, t1 * sin + t2 * cos], axis=1)


def _ret_span(q_ref, k_ref, v_ref, cos_ref, sin_ref, r_ref, dmat, cross, kdec, cdec, chunk_ids):
    c = RET_CHUNK
    staged = []
    for cid in chunk_ids:
        rows = pl.ds(pl.multiple_of(cid * c, c), c)
        cos, sin = cos_ref[rows, :], sin_ref[rows, :]
        qr = _rotate(q_ref[rows, :], cos, sin).astype(BF16)
        kr = _rotate(k_ref[rows, :], cos, sin) * (RET_DK ** -0.5)
        vb = v_ref[rows, :].astype(BF16)
        inner = _dot(qr, kr.astype(BF16), _NT)
        d_r = _dot((kr * kdec).astype(BF16), vb, _TN)
        staged.append((qr, vb, inner, d_r))
    intra = [_dot((inner * dmat).astype(BF16), vb) for _, vb, inner, _ in staged]
    outs = []
    r = r_ref[...]
    for (qr, _, _, d_r), o_intra in zip(staged, intra):
        outs.append(o_intra + cross * _dot(qr, r.astype(BF16)))
        r = cdec * r + d_r
    r_ref[...] = r
    return outs


def _ret_kernel(dm_f_ref, dm_b_ref, cr_f_ref, cr_b_ref, kd_f_ref, kd_b_ref, cd_ref,
                qf_ref, kf_ref, vf_ref, cf_ref, sf_ref, qb_ref, kb_ref, vb_ref, cb_ref, sb_ref,
                o_ref, rf_ref, rb_ref, *, tt):
    n = pl.program_id(2)
    per_block = tt // RET_CHUNK
    n_chunks = pl.num_programs(2) * per_block
    span = min(RET_SPAN, per_block)

    @pl.when(n == 0)
    def _():
        rf_ref[...] = jnp.zeros_like(rf_ref)
        rb_ref[...] = jnp.zeros_like(rb_ref)
        o_ref[...] = jnp.zeros_like(o_ref)

    cdec = cd_ref[...]

    def out_rows(chunk):
        return pl.ds(pl.multiple_of(chunk * RET_CHUNK, RET_CHUNK), RET_CHUNK)

    def body(r, carry):
        ids = [r * span + j for j in range(span)]
        o_f = _ret_span(qf_ref, kf_ref, vf_ref, cf_ref, sf_ref, rf_ref, dm_f_ref[...], cr_f_ref[...],
                        kd_f_ref[...], cdec, ids)
        o_b = _ret_span(qb_ref, kb_ref, vb_ref, cb_ref, sb_ref, rb_ref, dm_b_ref[...], cr_b_ref[...],
                        kd_b_ref[...], cdec, [per_block - 1 - i for i in ids])
        for j in range(span):
            it = n * per_block + ids[j]
            o_ref[out_rows(it), :] += o_f[j]
            o_ref[out_rows(n_chunks - 1 - it), :] += o_b[j]
        return carry

    lax.fori_loop(0, per_block // span, body, 0)


def _retention_consts():
    idx = np.arange(RET_CHUNK, dtype=np.float64)
    lg = np.log(1.0 - 2.0 ** (-5.0 - np.arange(RET_HEADS, dtype=np.float64)))[:, None, None]
    rel = idx[:, None] - idx[None, :]
    dm_f = np.where(rel >= 0, np.exp(np.maximum(rel, 0.0) * lg), 0.0)
    dm_b = np.transpose(dm_f, (0, 2, 1))
    wide = np.ones((1, 1, RET_DK))
    cr_f = np.exp((idx + 1.0)[None, :, None] * lg) * wide
    kd_f = np.exp((RET_CHUNK - 1.0 - idx)[None, :, None] * lg) * wide
    cd = np.exp(RET_CHUNK * lg) * wide
    as32 = lambda a: jnp.asarray(a.astype(np.float32))
    return (as32(dm_f), as32(dm_b), as32(cr_f), as32(cr_f[:, ::-1]), as32(kd_f), as32(kd_f[:, ::-1]), as32(cd))


def _retention(proj, cos, sin):
    b, l, _ = proj.shape
    tt = min(l, TIME_BLOCK)
    nb = l // tt
    consts = _retention_consts()

    def tpos(n, bwd):
        return nb - 1 - n if bwd else n

    def col(off, bwd):
        base = off // RET_DK
        return pl.BlockSpec((None, tt, RET_DK), lambda i, h, n: (i, tpos(n, bwd), base + h))

    def tab(bwd):
        return pl.BlockSpec((None, tt, RET_DK // 2), lambda i, h, n: (i, tpos(n, bwd), 0))

    def per_head(shape):
        return pl.BlockSpec((None,) + shape, lambda i, h, n: (h, 0, 0))

    sq, wide, row = (RET_CHUNK, RET_CHUNK), (RET_CHUNK, RET_DK), (1, RET_DK)
    return pl.pallas_call(
        functools.partial(_ret_kernel, tt=tt),
        grid=(b, RET_HEADS, nb),
        in_specs=[per_head(sq), per_head(sq), per_head(wide), per_head(wide), per_head(wide), per_head(wide),
                  per_head(row),
                  col(OFF_RQ, False), col(OFF_RK, False), col(OFF_RV, False), tab(False), tab(False),
                  col(OFF_RQ, True), col(OFF_RK, True), col(OFF_RV, True), tab(True), tab(True)],
        out_specs=pl.BlockSpec((None, l, RET_DV), lambda i, h, n: (i, 0, h)),
        out_shape=jax.ShapeDtypeStruct((b, l, RET_HEADS * RET_DV), F32),
        scratch_shapes=[pltpu.VMEM((RET_DK, RET_DV), F32), pltpu.VMEM((RET_DK, RET_DV), F32)],
        compiler_params=pltpu.CompilerParams(dimension_semantics=("parallel", "parallel", "arbitrary"),
                                             vmem_limit_bytes=VMEM_LIMIT),
        name="retention",
    )(*consts, proj, proj, proj, cos, sin, proj, proj, proj, cos, sin)


def _norm_heads(o, n_heads):
    d = o.shape[1] // n_heads
    parts = []
    for h in range(n_heads):
        oh = o[:, h * d:(h + 1) * d]
        parts.append(oh * lax.rsqrt(jnp.mean(oh * oh, -1, keepdims=True) + RMS_EPS))
    return jnp.concatenate(parts, axis=1)


def _mix_kernel(ohg_ref, hgate_ref, oret_ref, rgate_ref, ga_ref, gb_ref, x_ref, ghg_ref, gret_ref,
                wbh_ref, wbr_ref, wo_ref, lng_ref, lnb_ref, wr_ref, x1_ref, logit_ref):
    a = _norm_heads(ohg_ref[...], HG_HEADS) * ghg_ref[...] * _silu(hgate_ref[...])
    y_hg = _dot(a.astype(BF16), wbh_ref[...])
    c = _norm_heads(oret_ref[...], RET_HEADS) * gret_ref[...] * _silu(rgate_ref[...])
    y_ret = _dot(c.astype(BF16), wbr_ref[...])
    merged = _sigmoid(ga_ref[...]) * y_hg + _sigmoid(gb_ref[...]) * y_ret
    mix = _dot(merged.astype(BF16), wo_ref[...])
    x1 = _layer_norm(DEEPNORM_ALPHA * x_ref[...] + mix, lng_ref[...], lnb_ref[...])
    x1_ref[...] = x1
    logit_ref[...] = jnp.dot(x1, wr_ref[...], precision=lax.Precision.HIGHEST, preferred_element_type=F32)


def _mixer_out(o_hg, o_ret, proj, x, g_hg, g_ret, wbh, wbr, wo, ln_g, ln_b, w_route):
    t = x.shape[0]
    tm = min(t, MIX_TILE)
    d = D_MODEL
    rows = lambda cb: pl.BlockSpec((tm, d), lambda i: (i, cb))
    full = lambda shape: pl.BlockSpec(shape, lambda i: (0, 0))
    return pl.pallas_call(
        _mix_kernel,
        grid=(t // tm,),
        in_specs=[rows(0), rows(OFF_HGATE // d), rows(0), rows(OFF_RGATE // d), rows(OFF_GA // d),
                  rows(OFF_GB // d), rows(0), full((1, d)), full((1, d)),
                  full((d, d)), full((d, d)), full((d, d)), full((1, d)), full((1, d)), full((d, LANES))],
        out_specs=[pl.BlockSpec((tm, d), lambda i: (i, 0)), pl.BlockSpec((tm, LANES), lambda i: (i, 0))],
        out_shape=[jax.ShapeDtypeStruct((t, d), F32), jax.ShapeDtypeStruct((t, LANES), F32)],
        compiler_params=pltpu.CompilerParams(dimension_semantics=("parallel",), vmem_limit_bytes=VMEM_LIMIT),
        name="mixer_out",
    )(o_hg, proj, o_ret, proj, proj, proj, x, g_hg, g_ret, wbh, wbr, wo, ln_g, ln_b, w_route)


GROUP_LANE0 = N_EXPERTS


def _route_kernel(logit_ref, bias_ref, tri_ref, idx_ref, wt_ref, cnt_ref, carry_ref):
    i = pl.program_id(0)

    @pl.when(i == 0)
    def _():
        carry_ref[...] = jnp.zeros_like(carry_ref)

    lg = logit_ref[...] + bias_ref[...]
    tm = lg.shape[0]
    lane = lax.broadcasted_iota(jnp.int32, (tm, LANES), 1)
    first_of = lambda hit: jnp.min(jnp.where(hit, lane, LANES), -1, keepdims=True)

    g_mask = (lane >= GROUP_LANE0) & (lane < GROUP_LANE0 + N_GROUPS)
    g_l = jnp.where(g_mask, lg, NEG_BIG)
    g_max = jnp.max(g_l, -1, keepdims=True)
    grp = first_of(g_l == g_max) - GROUP_LANE0
    p_grp = 1.0 / jnp.sum(jnp.where(g_mask, jnp.exp(g_l - g_max), 0.0), -1, keepdims=True)

    e_lo = grp * EXPERTS_PER_GROUP
    e_l = jnp.where((lane >= e_lo) & (lane < e_lo + EXPERTS_PER_GROUP), lg, NEG_BIG)
    m1 = jnp.max(e_l, -1, keepdims=True)
    i1 = first_of(e_l == m1)
    e_l2 = jnp.where(lane == i1, NEG_BIG, e_l)
    m2 = jnp.max(e_l2, -1, keepdims=True)
    i2 = first_of(e_l2 == m2)
    r = jnp.exp(m2 - m1)
    w1 = p_grp / (1.0 + r)
    w2 = p_grp * r / (1.0 + r)

    hit1 = (lane == i1).astype(F32)
    hit2 = (lane == i2).astype(F32)
    both = hit1 + hit2
    before = _dot(tri_ref[...], both.astype(BF16)) + carry_ref[...]
    r1 = jnp.sum(hit1 * before, -1, keepdims=True).astype(jnp.int32)
    r2 = jnp.sum(hit2 * before, -1, keepdims=True).astype(jnp.int32)
    carry_ref[...] += jnp.sum(both, 0, keepdims=True)
    cnt_ref[...] = carry_ref[...]

    slot = lax.broadcasted_iota(jnp.int32, (tm, 8), 1)
    idx_ref[...] = jnp.where(slot == 0, i1, jnp.where(slot == 1, i2, jnp.where(slot == 2, r1, r2)))
    wt_ref[...] = jnp.where(slot == 0, w1, jnp.where(slot == 1, w2, 0.0))


def _route(logits, bias):
    t = logits.shape[0]
    tm = min(t, ROUTE_TILE)
    tri = jnp.asarray(np.tril(np.ones((tm, tm), np.float32), -1)).astype(BF16)
    return pl.pallas_call(
        _route_kernel,
        grid=(t // tm,),
        in_specs=[pl.BlockSpec((tm, LANES), lambda i: (i, 0)), pl.BlockSpec((1, LANES), lambda i: (0, 0)),
                  pl.BlockSpec((tm, tm), lambda i: (0, 0))],
        out_specs=[pl.BlockSpec((tm, 8), lambda i: (i, 0)), pl.BlockSpec((tm, 8), lambda i: (i, 0)),
                   pl.BlockSpec((1, LANES), lambda i: (0, 0))],
        out_shape=[jax.ShapeDtypeStruct((t, 8), jnp.int32), jax.ShapeDtypeStruct((t, 8), F32),
                   jax.ShapeDtypeStruct((1, LANES), F32)],
        scratch_shapes=[pltpu.VMEM((1, LANES), F32)],
        compiler_params=pltpu.CompilerParams(dimension_semantics=("arbitrary",)),
        name="route",
    )(logits, bias, tri)


def _row_copy(src_hbm, row, dst, dst_row, sem):
    return pltpu.make_async_copy(src_hbm.at[pl.ds(row, 1)], dst.at[pl.ds(dst_row, 1)], sem)


def _expert_kernel(be_ref, rows_ref, nused_ref, x_hbm, w1_ref, w3_ref, w2_ref, y_ref, buf, sem):
    i = pl.program_id(0)
    n_used = nused_ref[0]
    blk = buf.shape[1]

    def start(block, slot):
        def body(r, carry):
            _row_copy(x_hbm, rows_ref[block * blk + r], buf.at[slot], r, sem.at[slot]).start()
            return carry
        lax.fori_loop(0, blk, body, 0)

    def wait(slot):
        def body(r, carry):
            _row_copy(x_hbm, 0, buf.at[slot], r, sem.at[slot]).wait()
            return carry
        lax.fori_loop(0, blk, body, 0)

    @pl.when(i == 0)
    def _():
        start(0, 0)

    @pl.when(i + 1 < n_used)
    def _():
        start(i + 1, (i + 1) % 2)

    @pl.when(i < n_used)
    def _():
        slot = i % 2
        wait(slot)
        xs = buf[slot].astype(BF16)
        h = _silu(_dot(xs, w1_ref[...])) * _dot(xs, w3_ref[...])
        y_ref[...] = _dot(h.astype(BF16), w2_ref[...])

    @pl.when(i >= n_used)
    def _():
        y_ref[...] = jnp.zeros_like(y_ref)


def _experts(x1, block_expert, rows, n_used, w1, w3, w2, blk):
    n_blocks = block_expert.shape[0]
    d, de = D_MODEL, D_EXPERT
    grid_spec = pltpu.PrefetchScalarGridSpec(
        num_scalar_prefetch=3,
        grid=(n_blocks,),
        in_specs=[pl.BlockSpec(memory_space=pl.ANY),
                  pl.BlockSpec((None, d, de), lambda i, be, rw, nu: (be[i], 0, 0)),
                  pl.BlockSpec((None, d, de), lambda i, be, rw, nu: (be[i], 0, 0)),
                  pl.BlockSpec((None, de, d), lambda i, be, rw, nu: (be[i], 0, 0))],
        out_specs=pl.BlockSpec((blk, d), lambda i, be, rw, nu: (i, 0)),
        scratch_shapes=[pltpu.VMEM((2, blk, d), F32), pltpu.SemaphoreType.DMA((2,))],
    )
    return pl.pallas_call(
        _expert_kernel,
        grid_spec=grid_spec,
        out_shape=jax.ShapeDtypeStruct((n_blocks * blk, d), F32),
        compiler_params=pltpu.CompilerParams(dimension_semantics=("arbitrary",), vmem_limit_bytes=VMEM_LIMIT),
        name="experts",
    )(block_expert, rows, n_used, x1, w1, w3, w2)


def _combine_kernel(dest_ref, y_hbm, x1_ref, wt_ref, lng_ref, lnb_ref, o_ref, buf, sem):
    i = pl.program_id(0)
    n = pl.num_programs(0)
    tm = x1_ref.shape[0]

    def start(tile, slot):
        def body(r, carry):
            for k in range(TOP_K):
                _row_copy(y_hbm, dest_ref[(tile * tm + r) * TOP_K + k], buf.at[slot, k], r, sem.at[slot]).start()
            return carry
        lax.fori_loop(0, tm, body, 0)

    def wait(slot):
        def body(r, carry):
            for k in range(TOP_K):
                _row_copy(y_hbm, 0, buf.at[slot, k], r, sem.at[slot]).wait()
            return carry
        lax.fori_loop(0, tm, body, 0)

    @pl.when(i == 0)
    def _():
        start(0, 0)

    @pl.when(i + 1 < n)
    def _():
        start(i + 1, (i + 1) % 2)

    slot = i % 2
    wait(slot)
    wt = wt_ref[...]
    moe = wt[:, 0:1] * buf[slot, 0] + wt[:, 1:2] * buf[slot, 1]
    o_ref[...] = _layer_norm(DEEPNORM_ALPHA * x1_ref[...] + moe, lng_ref[...], lnb_ref[...])


def _combine(dest, y, x1, wts, ln_g, ln_b):
    t, d = x1.shape
    tm = min(t, ROUTE_TILE)
    grid_spec = pltpu.PrefetchScalarGridSpec(
        num_scalar_prefetch=1,
        grid=(t // tm,),
        in_specs=[pl.BlockSpec(memory_space=pl.ANY),
                  pl.BlockSpec((tm, d), lambda i, ds: (i, 0)),
                  pl.BlockSpec((tm, 8), lambda i, ds: (i, 0)),
                  pl.BlockSpec((1, d), lambda i, ds: (0, 0)),
                  pl.BlockSpec((1, d), lambda i, ds: (0, 0))],
        out_specs=pl.BlockSpec((tm, d), lambda i, ds: (i, 0)),
        scratch_shapes=[pltpu.VMEM((2, TOP_K, tm, d), F32), pltpu.SemaphoreType.DMA((2,))],
    )
    return pl.pallas_call(
        _combine_kernel,
        grid_spec=grid_spec,
        out_shape=jax.ShapeDtypeStruct((t, d), F32),
        compiler_params=pltpu.CompilerParams(dimension_semantics=("arbitrary",), vmem_limit_bytes=VMEM_LIMIT),
        name="combine",
    )(dest, y, x1, wts, ln_g, ln_b)


def _deinterleave_cols(w):
    dk = RET_DK
    wh = w.reshape(w.shape[0], RET_HEADS, dk // 2, 2)
    return jnp.concatenate([wh[..., 0], wh[..., 1]], axis=-1).reshape(w.shape[0], RET_HEADS * dk)


def kernel(x, positions, w_in, hg_lb_logits, hg_norm_g, ret_norm_g, w_branch_hg, w_branch_ret, w_out,
           ln1_g, ln1_b, w_group, b_group, w_router, b_router, w1, w3, w2, ln2_g, ln2_b):
    b, l, d = x.shape
    t = b * l
    lb_cum = jnp.cumsum(jax.nn.softmax(hg_lb_logits.astype(F32), axis=0), axis=0)
    cos, sin = _rope_tables(positions)
    xf = x.reshape(t, d)
    for layer in range(DEPTH):
        lb = (lb_cum[layer + 1] - lb_cum[0]).reshape(1, -1)
        w = w_in[layer]
        w = jnp.concatenate([w[:, :OFF_RQ], _deinterleave_cols(w[:, OFF_RQ:OFF_RK]),
                             _deinterleave_cols(w[:, OFF_RK:OFF_RV]), w[:, OFF_RV:]], axis=1)
        proj = _matmul(xf.astype(BF16), w.astype(BF16), F32, 1024, 1024, "in_proj")
        proj3 = proj.reshape(b, l, PROJ_DIM)
        o_hg = _hgrn2(proj3, lb).reshape(t, d)
        o_ret = _retention(proj3, cos, sin).reshape(t, d)

        w_route = jnp.zeros((d, LANES), F32)
        w_route = w_route.at[:, :N_EXPERTS].set(w_router[layer]).at[:, GROUP_LANE0:GROUP_LANE0 + N_GROUPS].set(w_group[layer])
        b_route = jnp.zeros((1, LANES), F32)
        b_route = b_route.at[0, :N_EXPERTS].set(b_router[layer]).at[0, GROUP_LANE0:GROUP_LANE0 + N_GROUPS].set(b_group[layer])
        row = lambda v: v.reshape(1, -1)
        x1, logits = _mixer_out(o_hg, o_ret, proj, xf, row(hg_norm_g[layer]), row(ret_norm_g[layer]),
                                w_branch_hg[layer].astype(BF16), w_branch_ret[layer].astype(BF16),
                                w_out[layer].astype(BF16), row(ln1_g[layer]), row(ln1_b[layer]), w_route)

        idx, wts, counts = _route(logits, b_route)
        blk = MOE_BLOCK
        counts = counts[0, :N_EXPERTS].astype(jnp.int32)
        padded = (counts + blk - 1) // blk * blk
        pend = jnp.cumsum(padded)
        pstart = pend - padded
        dest = pstart[idx[:, :TOP_K]] + idx[:, TOP_K:2 * TOP_K]
        n_blocks = t * TOP_K // blk + N_EXPERTS
        tok = jnp.broadcast_to(jnp.arange(t, dtype=jnp.int32)[:, None], (t, TOP_K))
        rows = jnp.zeros((n_blocks * blk,), jnp.int32).at[dest.reshape(-1)].set(tok.reshape(-1))
        block_row0 = jnp.arange(n_blocks, dtype=jnp.int32) * blk
        block_expert = jnp.minimum(jnp.sum((pend[None, :] <= block_row0[:, None]).astype(jnp.int32), axis=1),
                                   N_EXPERTS - 1)
        n_used = (pend[-1:] // blk).astype(jnp.int32)
        y = _experts(x1, block_expert, rows, n_used, w1[layer].astype(BF16), w3[layer].astype(BF16),
                     w2[layer].astype(BF16), blk)
        xf = _combine(dest.reshape(-1).astype(jnp.int32), y, x1, wts, row(ln2_g[layer]), row(ln2_b[layer]))
    return xf.reshape(b, l, d)
```

```python
import functools

import numpy as np
import jax
import jax.numpy as jnp
from jax import lax
from jax.experimental import pallas as pl
from jax.experimental.pallas import tpu as pltpu

D_MODEL = 1024
HG_HEADS = 8
HG_DK = 128
HG_DV = D_MODEL // HG_HEADS
RET_HEADS = 4
RET_DK = 256
RET_DV = 256
ROPE_BASE = 10000.0
OFF_HQ = 0
OFF_HF_FWD = 1024
OFF_HF_BWD = 2048
OFF_HI = 3072
OFF_HGATE = 4096
OFF_RQ = 5120
OFF_RK = 6144
OFF_RV = 7168
OFF_RGATE = 8192
OFF_GA = 9216
OFF_GB = 10240
PROJ_DIM = 11264
N_GROUPS = 4
EXPERTS_PER_GROUP = 8
N_EXPERTS = N_GROUPS * EXPERTS_PER_GROUP
TOP_K = 2
D_EXPERT = 512
DEPTH = 1
DEEPNORM_ALPHA = (2 * DEPTH) ** 0.25
LN_EPS = 1e-5
RMS_EPS = 1e-6

LANES = 128
CHUNK = 64
RET_CHUNK = 128
RET_SPAN = 2
SUB = 16
TIME_BLOCK = 512
MOE_BLOCK = 256
ROUTE_TILE = 256
DMA_UNROLL = 8
MIX_TILE = 256
VMEM_LIMIT = 56 * 1024 * 1024
NEG_BIG = -1e30
FAST_MIN_GATE = float(np.exp(-60.0 / SUB))
GROW_CAP = 64.0
FAST_SPAN = 4

F32 = jnp.float32
BF16 = jnp.bfloat16
_NT = (((1,), (1,)), ((), ()))
_TN = (((0,), (0,)), ((), ()))


def _sigmoid(v):
    return 1.0 / (1.0 + jnp.exp(-v))


def _silu(v):
    return v * _sigmoid(v)


def _dot(a, b, dims=None):
    if dims is None:
        return jnp.dot(a, b, preferred_element_type=F32)
    return lax.dot_general(a, b, dims, preferred_element_type=F32)


def _layer_norm(z, g, b):
    mu = jnp.mean(z, -1, keepdims=True)
    zc = z - mu
    var = jnp.mean(zc * zc, -1, keepdims=True)
    return zc * lax.rsqrt(var + LN_EPS) * g + b


def _rope_kernel(pos_ref, inv_ref, cos_ref, sin_ref):
    ang = pos_ref[...] * inv_ref[...]
    cos_ref[...] = jnp.cos(ang)
    sin_ref[...] = jnp.sin(ang)


def _rope_tables(positions):
    b, l = positions.shape
    tt = min(l, 1024)
    half = RET_DK // 2
    inv = (1.0 / (ROPE_BASE ** jnp.linspace(0.0, 1.0, half, dtype=F32))).reshape(1, half)
    pos = positions.astype(F32).reshape(b, l, 1)
    spec = pl.BlockSpec((None, tt, half), lambda i, j: (i, j, 0))
    return pl.pallas_call(
        _rope_kernel,
        grid=(b, l // tt),
        in_specs=[pl.BlockSpec((None, tt, 1), lambda i, j: (i, j, 0)),
                  pl.BlockSpec((1, half), lambda i, j: (0, 0))],
        out_specs=[spec, spec],
        out_shape=[jax.ShapeDtypeStruct((b, l, half), F32)] * 2,
        compiler_params=pltpu.CompilerParams(dimension_semantics=("parallel", "parallel")),
        name="rope_tables",
    )(pos, inv)


def _matmul_kernel(x_ref, w_ref, o_ref):
    o_ref[...] = _dot(x_ref[...], w_ref[...]).astype(o_ref.dtype)


def _matmul(x, w, out_dtype, tm, tn, name):
    m, k = x.shape
    n = w.shape[1]
    tm, tn = min(tm, m), min(tn, n)
    return pl.pallas_call(
        _matmul_kernel,
        grid=(n // tn, m // tm),
        in_specs=[pl.BlockSpec((tm, k), lambda j, i: (i, 0)),
                  pl.BlockSpec((k, tn), lambda j, i: (0, j))],
        out_specs=pl.BlockSpec((tm, tn), lambda j, i: (i, j)),
        out_shape=jax.ShapeDtypeStruct((m, n), out_dtype),
        compiler_params=pltpu.CompilerParams(dimension_semantics=("parallel", "parallel"),
                                             vmem_limit_bytes=VMEM_LIMIT),
        name=name,
    )(x, w)


def _hgrn_chunk(hq, hf, v, lb, s_t, tri, rev):
    c = hq.shape[0]
    q = _silu(hq)
    f = lb + (1.0 - lb) * _sigmoid(hf)
    k = 1.0 - f
    lf = jnp.log(f)
    g = jnp.dot(tri, lf, precision=lax.Precision.HIGHEST, preferred_element_type=F32)
    g_excl = g - lf
    g_tot = g[0:1] if rev else g[c - 1:c]

    vb = v.astype(BF16)
    o_inter = _dot((q * jnp.exp(g)).astype(BF16), s_t.astype(BF16), _NT)
    k_dec = (k * jnp.exp(g_tot - g)).astype(BF16)
    s_new = jnp.exp(g_tot) * s_t + _dot(vb, k_dec, _TN)

    ones = jnp.ones((LANES, LANES), BF16)
    t_idx = lax.broadcasted_iota(jnp.int32, (SUB, LANES), 0)
    outs = []
    for i in range(c // SUB):
        r0 = i * SUB
        q_i, k_i, g_i, v_i = q[r0:r0 + SUB], k[r0:r0 + SUB], g[r0:r0 + SUB], v[r0:r0 + SUB]
        lo, hi = (r0 + SUB, c) if rev else (0, r0)
        acc = jnp.zeros((SUB, v.shape[1]), F32)
        if hi > lo:
            first = r0 + SUB - 1 if rev else r0
            g_ref = g_excl[first:first + 1]
            q_t = (q_i * jnp.exp(g_i - g_ref)).astype(BF16)
            k_t = (k[lo:hi] * jnp.exp(g_ref - g[lo:hi])).astype(BF16)
            a = _dot(q_t, k_t, _NT)
            acc = _dot(a.astype(BF16), vb[lo:hi])
        prods = []
        for j in range(SUB):
            valid = (t_idx <= j) if rev else (t_idx >= j)
            e = jnp.exp(jnp.where(valid, g_i - g_i[j:j + 1], NEG_BIG))
            prods.append(q_i * k_i[j:j + 1] * e)
        a_rep = _dot(jnp.concatenate(prods, axis=0).astype(BF16), ones)
        for j in range(SUB):
            acc = acc + a_rep[j * SUB:(j + 1) * SUB] * v_i[j:j + 1]
        outs.append(acc)
    return o_inter + jnp.concatenate(outs, axis=0), s_new


def _hgrn_fast_span(lb, tri, allowed, q_ref, f_ref, v_ref, s_ref, chunk_ids, rev):
    c = CHUNK
    tri_b16 = tri.astype(BF16)
    prepped = []
    for cid in chunk_ids:
        rows = pl.ds(pl.multiple_of(cid * c, c), c)
        q = _silu(q_ref[rows, :])
        f = lb + (1.0 - lb) * _sigmoid(f_ref[rows, :])
        lf = jnp.log(f)
        hi = lf.astype(BF16)
        lo = (lf - hi.astype(F32)).astype(BF16)
        g2 = _dot(tri_b16, jnp.concatenate([hi, lo], axis=1))
        g = g2[:, :LANES] + g2[:, LANES:]
        prepped.append((q, 1.0 - f, lf, g, v_ref[rows, :].astype(BF16)))

    staged = []
    for q, k, lf, g, vb in prepped:
        eg = jnp.exp(g)
        total = eg[0:1] if rev else eg[c - 1:c]
        g_tot = g[0:1] if rev else g[c - 1:c]
        q_inter = (q * eg).astype(BF16)
        d_s = _dot(vb, (k * jnp.exp(g_tot - g)).astype(BF16), _TN)
        g_excl = g - lf
        scores = []
        for i in range(c // SUB):
            r0 = i * SUB
            first = r0 + SUB - 1 if rev else r0
            g_ref = g_excl[first:first + 1]
            q_t = (q[r0:r0 + SUB] * jnp.exp(g[r0:r0 + SUB] - g_ref)).astype(BF16)
            k_t = (k * jnp.exp(jnp.minimum(g_ref - g, GROW_CAP))).astype(BF16)
            scores.append(_dot(q_t, k_t, _NT))
        staged.append((q_inter, d_s, total, jnp.concatenate(scores, axis=0), vb))

    intra = [_dot((sc * allowed).astype(BF16), vb) for _, _, _, sc, vb in staged]

    outs = []
    s_t = s_ref[...]
    for (q_inter, d_s, total, _, _), o_intra in zip(staged, intra):
        outs.append(_dot(q_inter, s_t.astype(BF16), _NT) + o_intra)
        s_t = total * s_t + d_s
    s_ref[...] = s_t
    return outs


def _hgrn_kernel(lb_ref, tri_f_ref, tri_b_ref, qf_ref, ff_ref, vf_ref, qb_ref, fb_ref, vb_ref,
                 o_ref, sf_ref, sb_ref, *, tt):
    n = pl.program_id(2)
    per_block = tt // CHUNK
    n_chunks = pl.num_programs(2) * per_block

    @pl.when(n == 0)
    def _():
        sf_ref[...] = jnp.zeros_like(sf_ref)
        sb_ref[...] = jnp.zeros_like(sb_ref)
        o_ref[...] = jnp.zeros_like(o_ref)

    lb = lb_ref[...]

    def smallest_gate(f_ref):
        return jnp.min(lb + (1.0 - lb) * _sigmoid(jnp.min(f_ref[...], axis=0, keepdims=True)))

    fast_ok = jnp.minimum(smallest_gate(ff_ref), smallest_gate(fb_ref)) > FAST_MIN_GATE

    def out_rows(chunk):
        return pl.ds(pl.multiple_of(chunk * CHUNK, CHUNK), CHUNK)

    @pl.when(fast_ok)
    def _():
        span = min(FAST_SPAN, per_block)

        def body(r, carry):
            ids = [r * span + j for j in range(span)]
            o_f = _hgrn_fast_span(lb, tri_f_ref[...], tri_f_ref[...], qf_ref, ff_ref, vf_ref, sf_ref, ids, False)
            o_b = _hgrn_fast_span(lb, tri_b_ref[...], tri_b_ref[...], qb_ref, fb_ref, vb_ref, sb_ref,
                                  [per_block - 1 - i for i in ids], True)
            for j in range(span):
                it = n * per_block + ids[j]
                o_ref[out_rows(it), :] += o_f[j]
                o_ref[out_rows(n_chunks - 1 - it), :] += o_b[j]
            return carry

        lax.fori_loop(0, per_block // span, body, 0)

    @pl.when(jnp.logical_not(fast_ok))
    def _():
        def body(c, carry):
            it = n * per_block + c
            o_f, s_f = _hgrn_chunk(qf_ref[out_rows(c), :], ff_ref[out_rows(c), :], vf_ref[out_rows(c), :], lb,
                                   sf_ref[...], tri_f_ref[...], False)
            sf_ref[...] = s_f
            o_ref[out_rows(it), :] += o_f
            cb = per_block - 1 - c
            o_b, s_b = _hgrn_chunk(qb_ref[out_rows(cb), :], fb_ref[out_rows(cb), :], vb_ref[out_rows(cb), :], lb,
                                   sb_ref[...], tri_b_ref[...], True)
            sb_ref[...] = s_b
            o_ref[out_rows(n_chunks - 1 - it), :] += o_b
            return carry

        lax.fori_loop(0, per_block, body, 0)


def _hgrn2(proj, lb):
    b, l, _ = proj.shape
    tt = min(l, TIME_BLOCK)
    nb = l // tt
    ones = np.ones((CHUNK, CHUNK), np.float32)
    tri_f, tri_b = jnp.asarray(np.tril(ones)), jnp.asarray(np.triu(ones))

    def col(off, bwd):
        base = off // LANES
        if bwd:
            return pl.BlockSpec((None, tt, LANES), lambda i, h, n: (i, nb - 1 - n, base + h))
        return pl.BlockSpec((None, tt, LANES), lambda i, h, n: (i, n, base + h))

    sq_spec = pl.BlockSpec((CHUNK, CHUNK), lambda i, h, n: (0, 0))
    return pl.pallas_call(
        functools.partial(_hgrn_kernel, tt=tt),
        grid=(b, HG_HEADS, nb),
        in_specs=[pl.BlockSpec((1, LANES), lambda i, h, n: (0, h)), sq_spec, sq_spec,
                  col(OFF_HQ, False), col(OFF_HF_FWD, False), col(OFF_HI, False),
                  col(OFF_HQ, True), col(OFF_HF_BWD, True), col(OFF_HI, True)],
        out_specs=pl.BlockSpec((None, l, HG_DV), lambda i, h, n: (i, 0, h)),
        out_shape=jax.ShapeDtypeStruct((b, l, HG_HEADS * HG_DV), F32),
        scratch_shapes=[pltpu.VMEM((HG_DV, HG_DK), F32), pltpu.VMEM((HG_DV, HG_DK), F32)],
        compiler_params=pltpu.CompilerParams(dimension_semantics=("parallel", "parallel", "arbitrary"),
                                             vmem_limit_bytes=VMEM_LIMIT),
        name="hgrn2",
    )(lb, tri_f, tri_b, proj, proj, proj, proj, proj, proj)


def _rotate(t, cos, sin):
    half = t.shape[1] // 2
    t1, t2 = t[:, :half], t[:, half:]
    return jnp.concatenate([t1 * cos - t2 * sin, t1 * sin + t2 * cos], axis=1)


def _ret_span(q_ref, k_ref, v_ref, cos_ref, sin_ref, r_ref, dmat, cross, kdec, cdec, chunk_ids):
    c = RET_CHUNK
    staged = []
    for cid in chunk_ids:
        rows = pl.ds(pl.multiple_of(cid * c, c), c)
        cos, sin = cos_ref[rows, :], sin_ref[rows, :]
        qr = _rotate(q_ref[rows, :], cos, sin).astype(BF16)
        kr = _rotate(k_ref[rows, :], cos, sin) * (RET_DK ** -0.5)
        vb = v_ref[rows, :].astype(BF16)
        inner = _dot(qr, kr.astype(BF16), _NT)
        d_r = _dot((kr * kdec).astype(BF16), vb, _TN)
        staged.append((qr, vb, inner, d_r))
    intra = [_dot((inner * dmat).astype(BF16), vb) for _, vb, inner, _ in staged]
    outs = []
    r = r_ref[...]
    for (qr, _, _, d_r), o_intra in zip(staged, intra):
        outs.append(o_intra + cross * _dot(qr, r.astype(BF16)))
        r = cdec * r + d_r
    r_ref[...] = r
    return outs


def _ret_kernel(dm_f_ref, dm_b_ref, cr_f_ref, cr_b_ref, kd_f_ref, kd_b_ref, cd_ref,
                qf_ref, kf_ref, vf_ref, cf_ref, sf_ref, qb_ref, kb_ref, vb_ref, cb_ref, sb_ref,
                o_ref, rf_ref, rb_ref, *, tt):
    n = pl.program_id(2)
    per_block = tt // RET_CHUNK
    n_chunks = pl.num_programs(2) * per_block
    span = min(RET_SPAN, per_block)

    @pl.when(n == 0)
    def _():
        rf_ref[...] = jnp.zeros_like(rf_ref)
        rb_ref[...] = jnp.zeros_like(rb_ref)
        o_ref[...] = jnp.zeros_like(o_ref)

    cdec = cd_ref[...]

    def out_rows(chunk):
        return pl.ds(pl.multiple_of(chunk * RET_CHUNK, RET_CHUNK), RET_CHUNK)

    def body(r, carry):
        ids = [r * span + j for j in range(span)]
        o_f = _ret_span(qf_ref, kf_ref, vf_ref, cf_ref, sf_ref, rf_ref, dm_f_ref[...], cr_f_ref[...],
                        kd_f_ref[...], cdec, ids)
        o_b = _ret_span(qb_ref, kb_ref, vb_ref, cb_ref, sb_ref, rb_ref, dm_b_ref[...], cr_b_ref[...],
                        kd_b_ref[...], cdec, [per_block - 1 - i for i in ids])
        for j in range(span):
            it = n * per_block + ids[j]
            o_ref[out_rows(it), :] += o_f[j]
            o_ref[out_rows(n_chunks - 1 - it), :] += o_b[j]
        return carry

    lax.fori_loop(0, per_block // span, body, 0)


def _retention_consts():
    idx = np.arange(RET_CHUNK, dtype=np.float64)
    lg = np.log(1.0 - 2.0 ** (-5.0 - np.arange(RET_HEADS, dtype=np.float64)))[:, None, None]
    rel = idx[:, None] - idx[None, :]
    dm_f = np.where(rel >= 0, np.exp(np.maximum(rel, 0.0) * lg), 0.0)
    dm_b = np.transpose(dm_f, (0, 2, 1))
    wide = np.ones((1, 1, RET_DK))
    cr_f = np.exp((idx + 1.0)[None, :, None] * lg) * wide
    kd_f = np.exp((RET_CHUNK - 1.0 - idx)[None, :, None] * lg) * wide
    cd = np.exp(RET_CHUNK * lg) * wide
    as32 = lambda a: jnp.asarray(a.astype(np.float32))
    return (as32(dm_f), as32(dm_b), as32(cr_f), as32(cr_f[:, ::-1]), as32(kd_f), as32(kd_f[:, ::-1]), as32(cd))


def _retention(proj, cos, sin):
    b, l, _ = proj.shape
    tt = min(l, TIME_BLOCK)
    nb = l // tt
    consts = _retention_consts()

    def tpos(n, bwd):
        return nb - 1 - n if bwd else n

    def col(off, bwd):
        base = off // RET_DK
        return pl.BlockSpec((None, tt, RET_DK), lambda i, h, n: (i, tpos(n, bwd), base + h))

    def tab(bwd):
        return pl.BlockSpec((None, tt, RET_DK // 2), lambda i, h, n: (i, tpos(n, bwd), 0))

    def per_head(shape):
        return pl.BlockSpec((None,) + shape, lambda i, h, n: (h, 0, 0))

    sq, wide, row = (RET_CHUNK, RET_CHUNK), (RET_CHUNK, RET_DK), (1, RET_DK)
    return pl.pallas_call(
        functools.partial(_ret_kernel, tt=tt),
        grid=(b, RET_HEADS, nb),
        in_specs=[per_head(sq), per_head(sq), per_head(wide), per_head(wide), per_head(wide), per_head(wide),
                  per_head(row),
                  col(OFF_RQ, False), col(OFF_RK, False), col(OFF_RV, False), tab(False), tab(False),
                  col(OFF_RQ, True), col(OFF_RK, True), col(OFF_RV, True), tab(True), tab(True)],
        out_specs=pl.BlockSpec((None, l, RET_DV), lambda i, h, n: (i, 0, h)),
        out_shape=jax.ShapeDtypeStruct((b, l, RET_HEADS * RET_DV), F32),
        scratch_shapes=[pltpu.VMEM((RET_DK, RET_DV), F32), pltpu.VMEM((RET_DK, RET_DV), F32)],
        compiler_params=pltpu.CompilerParams(dimension_semantics=("parallel", "parallel", "arbitrary"),
                                             vmem_limit_bytes=VMEM_LIMIT),
        name="retention",
    )(*consts, proj, proj, proj, cos, sin, proj, proj, proj, cos, sin)


def _norm_heads(o, n_heads):
    d = o.shape[1] // n_heads
    parts = []
    for h in range(n_heads):
        oh = o[:, h * d:(h + 1) * d]
        parts.append(oh * lax.rsqrt(jnp.mean(oh * oh, -1, keepdims=True) + RMS_EPS))
    return jnp.concatenate(parts, axis=1)


def _mix_kernel(ohg_ref, hgate_ref, oret_ref, rgate_ref, ga_ref, gb_ref, x_ref, ghg_ref, gret_ref,
                wbh_ref, wbr_ref, wo_ref, lng_ref, lnb_ref, wr_ref, x1_ref, logit_ref):
    a = _norm_heads(ohg_ref[...], HG_HEADS) * ghg_ref[...] * _silu(hgate_ref[...])
    y_hg = _dot(a.astype(BF16), wbh_ref[...])
    c = _norm_heads(oret_ref[...], RET_HEADS) * gret_ref[...] * _silu(rgate_ref[...])
    y_ret = _dot(c.astype(BF16), wbr_ref[...])
    merged = _sigmoid(ga_ref[...]) * y_hg + _sigmoid(gb_ref[...]) * y_ret
    mix = _dot(merged.astype(BF16), wo_ref[...])
    x1 = _layer_norm(DEEPNORM_ALPHA * x_ref[...] + mix, lng_ref[...], lnb_ref[...])
    x1_ref[...] = x1
    logit_ref[...] = jnp.dot(x1, wr_ref[...], precision=lax.Precision.HIGHEST, preferred_element_type=F32)


def _mixer_out(o_hg, o_ret, proj, x, g_hg, g_ret, wbh, wbr, wo, ln_g, ln_b, w_route):
    t = x.shape[0]
    tm = min(t, MIX_TILE)
    d = D_MODEL
    rows = lambda cb: pl.BlockSpec((tm, d), lambda i: (i, cb))
    full = lambda shape: pl.BlockSpec(shape, lambda i: (0, 0))
    return pl.pallas_call(
        _mix_kernel,
        grid=(t // tm,),
        in_specs=[rows(0), rows(OFF_HGATE // d), rows(0), rows(OFF_RGATE // d), rows(OFF_GA // d),
                  rows(OFF_GB // d), rows(0), full((1, d)), full((1, d)),
                  full((d, d)), full((d, d)), full((d, d)), full((1, d)), full((1, d)), full((d, LANES))],
        out_specs=[pl.BlockSpec((tm, d), lambda i: (i, 0)), pl.BlockSpec((tm, LANES), lambda i: (i, 0))],
        out_shape=[jax.ShapeDtypeStruct((t, d), F32), jax.ShapeDtypeStruct((t, LANES), F32)],
        compiler_params=pltpu.CompilerParams(dimension_semantics=("parallel",), vmem_limit_bytes=VMEM_LIMIT),
        name="mixer_out",
    )(o_hg, proj, o_ret, proj, proj, proj, x, g_hg, g_ret, wbh, wbr, wo, ln_g, ln_b, w_route)


GROUP_LANE0 = N_EXPERTS


def _route_kernel(logit_ref, bias_ref, tri_ref, idx_ref, wt_ref, cnt_ref, carry_ref):
    i = pl.program_id(0)

    @pl.when(i == 0)
    def _():
        carry_ref[...] = jnp.zeros_like(carry_ref)

    lg = logit_ref[...] + bias_ref[...]
    tm = lg.shape[0]
    lane = lax.broadcasted_iota(jnp.int32, (tm, LANES), 1)
    first_of = lambda hit: jnp.min(jnp.where(hit, lane, LANES), -1, keepdims=True)

    g_mask = (lane >= GROUP_LANE0) & (lane < GROUP_LANE0 + N_GROUPS)
    g_l = jnp.where(g_mask, lg, NEG_BIG)
    g_max = jnp.max(g_l, -1, keepdims=True)
    grp = first_of(g_l == g_max) - GROUP_LANE0
    p_grp = 1.0 / jnp.sum(jnp.where(g_mask, jnp.exp(g_l - g_max), 0.0), -1, keepdims=True)

    e_lo = grp * EXPERTS_PER_GROUP
    e_l = jnp.where((lane >= e_lo) & (lane < e_lo + EXPERTS_PER_GROUP), lg, NEG_BIG)
    m1 = jnp.max(e_l, -1, keepdims=True)
    i1 = first_of(e_l == m1)
    e_l2 = jnp.where(lane == i1, NEG_BIG, e_l)
    m2 = jnp.max(e_l2, -1, keepdims=True)
    i2 = first_of(e_l2 == m2)
    r = jnp.exp(m2 - m1)
    w1 = p_grp / (1.0 + r)
    w2 = p_grp * r / (1.0 + r)

    hit1 = (lane == i1).astype(F32)
    hit2 = (lane == i2).astype(F32)
    both = hit1 + hit2
    before = _dot(tri_ref[...], both.astype(BF16)) + carry_ref[...]
    r1 = jnp.sum(hit1 * before, -1, keepdims=True).astype(jnp.int32)
    r2 = jnp.sum(hit2 * before, -1, keepdims=True).astype(jnp.int32)
    carry_ref[...] += jnp.sum(both, 0, keepdims=True)
    cnt_ref[...] = carry_ref[...]

    slot = lax.broadcasted_iota(jnp.int32, (tm, 8), 1)
    idx_ref[...] = jnp.where(slot == 0, i1, jnp.where(slot == 1, i2, jnp.where(slot == 2, r1, r2)))
    wt_ref[...] = jnp.where(slot == 0, w1, jnp.where(slot == 1, w2, 0.0))


def _route(logits, bias):
    t = logits.shape[0]
    tm = min(t, ROUTE_TILE)
    tri = jnp.asarray(np.tril(np.ones((tm, tm), np.float32), -1)).astype(BF16)
    return pl.pallas_call(
        _route_kernel,
        grid=(t // tm,),
        in_specs=[pl.BlockSpec((tm, LANES), lambda i: (i, 0)), pl.BlockSpec((1, LANES), lambda i: (0, 0)),
                  pl.BlockSpec((tm, tm), lambda i: (0, 0))],
        out_specs=[pl.BlockSpec((tm, 8), lambda i: (i, 0)), pl.BlockSpec((tm, 8), lambda i: (i, 0)),
                   pl.BlockSpec((1, LANES), lambda i: (0, 0))],
        out_shape=[jax.ShapeDtypeStruct((t, 8), jnp.int32), jax.ShapeDtypeStruct((t, 8), F32),
                   jax.ShapeDtypeStruct((1, LANES), F32)],
        scratch_shapes=[pltpu.VMEM((1, LANES), F32)],
        compiler_params=pltpu.CompilerParams(dimension_semantics=("arbitrary",)),
        name="route",
    )(logits, bias, tri)


def _row_copy(src, row, dst, dst_row, sem):
    return pltpu.make_async_copy(src.at[pl.ds(row, 1)], dst.at[pl.ds(dst_row, 1)], sem)


def _dispatch_kernel(dest_ref, x_hbm, xs_zero_hbm, xs_hbm, sem, *, tm):
    del xs_zero_hbm
    i = pl.program_id(0)
    n = pl.num_programs(0)

    def drain(slot):
        pltpu.make_async_copy(x_hbm.at[pl.ds(0, tm * TOP_K)], xs_hbm.at[pl.ds(0, tm * TOP_K)], sem.at[slot]).wait()

    def body(r, carry):
        t = i * tm + r
        for k in range(TOP_K):
            _row_copy(x_hbm, t, xs_hbm, dest_ref[t * TOP_K + k], sem.at[i % 2]).start()
        return carry

    lax.fori_loop(0, tm, body, 0, unroll=DMA_UNROLL)

    @pl.when(i > 0)
    def _():
        drain((i + 1) % 2)

    @pl.when(i == n - 1)
    def _():
        drain(i % 2)


def _dispatch(dest, x1, n_rows):
    t, d = x1.shape
    tm = min(t, ROUTE_TILE)
    grid_spec = pltpu.PrefetchScalarGridSpec(
        num_scalar_prefetch=1,
        grid=(t // tm,),
        in_specs=[pl.BlockSpec(memory_space=pl.ANY), pl.BlockSpec(memory_space=pl.ANY)],
        out_specs=pl.BlockSpec(memory_space=pl.ANY),
        scratch_shapes=[pltpu.SemaphoreType.DMA((2,))],
    )
    return pl.pallas_call(
        functools.partial(_dispatch_kernel, tm=tm),
        grid_spec=grid_spec,
        out_shape=jax.ShapeDtypeStruct((n_rows, d), F32),
        input_output_aliases={2: 0},
        compiler_params=pltpu.CompilerParams(dimension_semantics=("arbitrary",)),
        name="dispatch",
    )(dest, x1, jnp.zeros((n_rows, d), F32))


def _expert_kernel(be_ref, nused_ref, xs_ref, w1_ref, w3_ref, w2_ref, y_ref):
    i = pl.program_id(0)

    @pl.when(i < nused_ref[0])
    def _():
        xs = xs_ref[...].astype(BF16)
        h = _silu(_dot(xs, w1_ref[...])) * _dot(xs, w3_ref[...])
        y_ref[...] = _dot(h.astype(BF16), w2_ref[...])

    @pl.when(i >= nused_ref[0])
    def _():
        y_ref[...] = jnp.zeros_like(y_ref)


def _experts(xs, block_expert, n_used, w1, w3, w2, blk):
    n_blocks = block_expert.shape[0]
    d, de = D_MODEL, D_EXPERT
    grid_spec = pltpu.PrefetchScalarGridSpec(
        num_scalar_prefetch=2,
        grid=(n_blocks,),
        in_specs=[pl.BlockSpec((blk, d), lambda i, be, nu: (i, 0)),
                  pl.BlockSpec((None, d, de), lambda i, be, nu: (be[i], 0, 0)),
                  pl.BlockSpec((None, d, de), lambda i, be, nu: (be[i], 0, 0)),
                  pl.BlockSpec((None, de, d), lambda i, be, nu: (be[i], 0, 0))],
        out_specs=pl.BlockSpec((blk, d), lambda i, be, nu: (i, 0)),
    )
    return pl.pallas_call(
        _expert_kernel,
        grid_spec=grid_spec,
        out_shape=jax.ShapeDtypeStruct((n_blocks * blk, d), F32),
        compiler_params=pltpu.CompilerParams(dimension_semantics=("arbitrary",), vmem_limit_bytes=VMEM_LIMIT),
        name="experts",
    )(block_expert, n_used, xs, w1, w3, w2)


def _combine_kernel(dest_ref, y_hbm, x1_ref, wt_ref, lng_ref, lnb_ref, o_ref, buf, sem):
    i = pl.program_id(0)
    n = pl.num_programs(0)
    tm = x1_ref.shape[0]

    def start(tile, slot):
        def body(r, carry):
            for k in range(TOP_K):
                _row_copy(y_hbm, dest_ref[(tile * tm + r) * TOP_K + k], buf.at[slot, k], r, sem.at[slot]).start()
            return carry
        lax.fori_loop(0, tm, body, 0, unroll=DMA_UNROLL)

    def drain(slot):
        for k in range(TOP_K):
            pltpu.make_async_copy(y_hbm.at[pl.ds(0, tm)], buf.at[slot, k], sem.at[slot]).wait()

    @pl.when(i == 0)
    def _():
        start(0, 0)

    @pl.when(i + 1 < n)
    def _():
        start(i + 1, (i + 1) % 2)

    slot = i % 2
    drain(slot)
    wt = wt_ref[...]
    moe = wt[:, 0:1] * buf[slot, 0] + wt[:, 1:2] * buf[slot, 1]
    o_ref[...] = _layer_norm(DEEPNORM_ALPHA * x1_ref[...] + moe, lng_ref[...], lnb_ref[...])


def _combine(dest, y, x1, wts, ln_g, ln_b):
    t, d = x1.shape
    tm = min(t, ROUTE_TILE)
    grid_spec = pltpu.PrefetchScalarGridSpec(
        num_scalar_prefetch=1,
        grid=(t // tm,),
        in_specs=[pl.BlockSpec(memory_space=pl.ANY),
                  pl.BlockSpec((tm, d), lambda i, ds: (i, 0)),
                  pl.BlockSpec((tm, 8), lambda i, ds: (i, 0)),
                  pl.BlockSpec((1, d), lambda i, ds: (0, 0)),
                  pl.BlockSpec((1, d), lambda i, ds: (0, 0))],
        out_specs=pl.BlockSpec((tm, d), lambda i, ds: (i, 0)),
        scratch_shapes=[pltpu.VMEM((2, TOP_K, tm, d), F32), pltpu.SemaphoreType.DMA((2,))],
    )
    return pl.pallas_call(
        _combine_kernel,
        grid_spec=grid_spec,
        out_shape=jax.ShapeDtypeStruct((t, d), F32),
        compiler_params=pltpu.CompilerParams(dimension_semantics=("arbitrary",), vmem_limit_bytes=VMEM_LIMIT),
        name="combine",
    )(dest, y, x1, wts, ln_g, ln_b)


def _deinterleave_cols(w):
    dk = RET_DK
    wh = w.reshape(w.shape[0], RET_HEADS, dk // 2, 2)
    return jnp.concatenate([wh[..., 0], wh[..., 1]], axis=-1).reshape(w.shape[0], RET_HEADS * dk)


def kernel(x, positions, w_in, hg_lb_logits, hg_norm_g, ret_norm_g, w_branch_hg, w_branch_ret, w_out,
           ln1_g, ln1_b, w_group, b_group, w_router, b_router, w1, w3, w2, ln2_g, ln2_b):
    b, l, d = x.shape
    t = b * l
    lb_cum = jnp.cumsum(jax.nn.softmax(hg_lb_logits.astype(F32), axis=0), axis=0)
    cos, sin = _rope_tables(positions)
    xf = x.reshape(t, d)
    for layer in range(DEPTH):
        lb = (lb_cum[layer + 1] - lb_cum[0]).reshape(1, -1)
        w = w_in[layer]
        w = jnp.concatenate([w[:, :OFF_RQ], _deinterleave_cols(w[:, OFF_RQ:OFF_RK]),
                             _deinterleave_cols(w[:, OFF_RK:OFF_RV]), w[:, OFF_RV:]], axis=1)
        proj = _matmul(xf.astype(BF16), w.astype(BF16), F32, 1024, 1024, "in_proj")
        proj3 = proj.reshape(b, l, PROJ_DIM)
        o_hg = _hgrn2(proj3, lb).reshape(t, d)
        o_ret = _retention(proj3, cos, sin).reshape(t, d)

        w_route = jnp.zeros((d, LANES), F32)
        w_route = w_route.at[:, :N_EXPERTS].set(w_router[layer]).at[:, GROUP_LANE0:GROUP_LANE0 + N_GROUPS].set(w_group[layer])
        b_route = jnp.zeros((1, LANES), F32)
        b_route = b_route.at[0, :N_EXPERTS].set(b_router[layer]).at[0, GROUP_LANE0:GROUP_LANE0 + N_GROUPS].set(b_group[layer])
        row = lambda v: v.reshape(1, -1)
        x1, logits = _mixer_out(o_hg, o_ret, proj, xf, row(hg_norm_g[layer]), row(ret_norm_g[layer]),
                                w_branch_hg[layer].astype(BF16), w_branch_ret[layer].astype(BF16),
                                w_out[layer].astype(BF16), row(ln1_g[layer]), row(ln1_b[layer]), w_route)

        idx, wts, counts = _route(logits, b_route)
        blk = MOE_BLOCK
        counts = counts[0, :N_EXPERTS].astype(jnp.int32)
        padded = (counts + blk - 1) // blk * blk
        pend = jnp.cumsum(padded)
        pstart = pend - padded
        dest = pstart[idx[:, :TOP_K]] + idx[:, TOP_K:2 * TOP_K]
        n_blocks = t * TOP_K // blk + N_EXPERTS
        block_row0 = jnp.arange(n_blocks, dtype=jnp.int32) * blk
        block_expert = jnp.minimum(jnp.sum((pend[None, :] <= block_row0[:, None]).astype(jnp.int32), axis=1),
                                   N_EXPERTS - 1)
        n_used = (pend[-1:] // blk).astype(jnp.int32)
        dest = dest.reshape(-1).astype(jnp.int32)
        xs = _dispatch(dest, x1, n_blocks * blk)
        y = _experts(xs, block_expert, n_used, w1[layer].astype(BF16), w3[layer].astype(BF16),
                     w2[layer].astype(BF16), blk)
        xf = _combine(dest, y, x1, wts, row(ln2_g[layer]), row(ln2_b[layer]))
    return xf.reshape(b, l, d)
```

```python
import functools

import numpy as np
import jax
import jax.numpy as jnp
from jax import lax
from jax.experimental import pallas as pl
from jax.experimental.pallas import tpu as pltpu

D_MODEL = 1024
HG_HEADS = 8
HG_DK = 128
HG_DV = D_MODEL // HG_HEADS
RET_HEADS = 4
RET_DK = 256
RET_DV = 256
ROPE_BASE = 10000.0
OFF_HQ = 0
OFF_HF_FWD = 1024
OFF_HF_BWD = 2048
OFF_HI = 3072
OFF_HGATE = 4096
OFF_RQ = 5120
OFF_RK = 6144
OFF_RV = 7168
OFF_RGATE = 8192
OFF_GA = 9216
OFF_GB = 10240
PROJ_DIM = 11264
N_GROUPS = 4
EXPERTS_PER_GROUP = 8
N_EXPERTS = N_GROUPS * EXPERTS_PER_GROUP
TOP_K = 2
D_EXPERT = 512
DEPTH = 1
DEEPNORM_ALPHA = (2 * DEPTH) ** 0.25
LN_EPS = 1e-5
RMS_EPS = 1e-6

LANES = 128
CHUNK = 64
RET_CHUNK = 128
RET_SPAN = 2
SUB = 16
TIME_BLOCK = 512
MOE_BLOCK = 256
ROUTE_TILE = 256
DMA_UNROLL = 8
MIX_TILE = 256
VMEM_LIMIT = 56 * 1024 * 1024
NEG_BIG = -1e30
FAST_MIN_GATE = float(np.exp(-60.0 / SUB))
GROW_CAP = 64.0
FAST_SPAN = 4

F32 = jnp.float32
BF16 = jnp.bfloat16
_NT = (((1,), (1,)), ((), ()))
_TN = (((0,), (0,)), ((), ()))


def _sigmoid(v):
    return 1.0 / (1.0 + jnp.exp(-v))


def _silu(v):
    return v * _sigmoid(v)


def _dot(a, b, dims=None):
    if dims is None:
        return jnp.dot(a, b, preferred_element_type=F32)
    return lax.dot_general(a, b, dims, preferred_element_type=F32)


def _layer_norm(z, g, b):
    mu = jnp.mean(z, -1, keepdims=True)
    zc = z - mu
    var = jnp.mean(zc * zc, -1, keepdims=True)
    return zc * lax.rsqrt(var + LN_EPS) * g + b


def _rope_kernel(pos_ref, inv_ref, cos_ref, sin_ref):
    ang = pos_ref[...] * inv_ref[...]
    cos_ref[...] = jnp.cos(ang)
    sin_ref[...] = jnp.sin(ang)


def _rope_tables(positions):
    b, l = positions.shape
    tt = min(l, 1024)
    half = RET_DK // 2
    inv = (1.0 / (ROPE_BASE ** jnp.linspace(0.0, 1.0, half, dtype=F32))).reshape(1, half)
    pos = positions.astype(F32).reshape(b, l, 1)
    spec = pl.BlockSpec((None, tt, half), lambda i, j: (i, j, 0))
    return pl.pallas_call(
        _rope_kernel,
        grid=(b, l // tt),
        in_specs=[pl.BlockSpec((None, tt, 1), lambda i, j: (i, j, 0)),
                  pl.BlockSpec((1, half), lambda i, j: (0, 0))],
        out_specs=[spec, spec],
        out_shape=[jax.ShapeDtypeStruct((b, l, half), F32)] * 2,
        compiler_params=pltpu.CompilerParams(dimension_semantics=("parallel", "parallel")),
        name="rope_tables",
    )(pos, inv)


def _matmul_kernel(x_ref, w_ref, o_ref):
    o_ref[...] = _dot(x_ref[...], w_ref[...]).astype(o_ref.dtype)


def _matmul(x, w, out_dtype, tm, tn, name):
    m, k = x.shape
    n = w.shape[1]
    tm, tn = min(tm, m), min(tn, n)
    return pl.pallas_call(
        _matmul_kernel,
        grid=(n // tn, m // tm),
        in_specs=[pl.BlockSpec((tm, k), lambda j, i: (i, 0)),
                  pl.BlockSpec((k, tn), lambda j, i: (0, j))],
        out_specs=pl.BlockSpec((tm, tn), lambda j, i: (i, j)),
        out_shape=jax.ShapeDtypeStruct((m, n), out_dtype),
        compiler_params=pltpu.CompilerParams(dimension_semantics=("parallel", "parallel"),
                                             vmem_limit_bytes=VMEM_LIMIT),
        name=name,
    )(x, w)


def _hgrn_chunk(hq, hf, v, lb, s_t, tri, rev):
    c = hq.shape[0]
    q = _silu(hq)
    f = lb + (1.0 - lb) * _sigmoid(hf)
    k = 1.0 - f
    lf = jnp.log(f)
    g = jnp.dot(tri, lf, precision=lax.Precision.HIGHEST, preferred_element_type=F32)
    g_excl = g - lf
    g_tot = g[0:1] if rev else g[c - 1:c]

    vb = v.astype(BF16)
    o_inter = _dot((q * jnp.exp(g)).astype(BF16), s_t.astype(BF16), _NT)
    k_dec = (k * jnp.exp(g_tot - g)).astype(BF16)
    s_new = jnp.exp(g_tot) * s_t + _dot(vb, k_dec, _TN)

    ones = jnp.ones((LANES, LANES), BF16)
    t_idx = lax.broadcasted_iota(jnp.int32, (SUB, LANES), 0)
    outs = []
    for i in range(c // SUB):
        r0 = i * SUB
        q_i, k_i, g_i, v_i = q[r0:r0 + SUB], k[r0:r0 + SUB], g[r0:r0 + SUB], v[r0:r0 + SUB]
        lo, hi = (r0 + SUB, c) if rev else (0, r0)
        acc = jnp.zeros((SUB, v.shape[1]), F32)
        if hi > lo:
            first = r0 + SUB - 1 if rev else r0
            g_ref = g_excl[first:first + 1]
            q_t = (q_i * jnp.exp(g_i - g_ref)).astype(BF16)
            k_t = (k[lo:hi] * jnp.exp(g_ref - g[lo:hi])).astype(BF16)
            a = _dot(q_t, k_t, _NT)
            acc = _dot(a.astype(BF16), vb[lo:hi])
        prods = []
        for j in range(SUB):
            valid = (t_idx <= j) if rev else (t_idx >= j)
            e = jnp.exp(jnp.where(valid, g_i - g_i[j:j + 1], NEG_BIG))
            prods.append(q_i * k_i[j:j + 1] * e)
        a_rep = _dot(jnp.concatenate(prods, axis=0).astype(BF16), ones)
        for j in range(SUB):
            acc = acc + a_rep[j * SUB:(j + 1) * SUB] * v_i[j:j + 1]
        outs.append(acc)
    return o_inter + jnp.concatenate(outs, axis=0), s_new


def _hgrn_fast_span(lb, tri, allowed, q_ref, f_ref, v_ref, s_ref, chunk_ids, rev):
    c = CHUNK
    tri_b16 = tri.astype(BF16)
    prepped = []
    for cid in chunk_ids:
        rows = pl.ds(pl.multiple_of(cid * c, c), c)
        q = _silu(q_ref[rows, :])
        f = lb + (1.0 - lb) * _sigmoid(f_ref[rows, :])
        lf = jnp.log(f)
        hi = lf.astype(BF16)
        lo = (lf - hi.astype(F32)).astype(BF16)
        g2 = _dot(tri_b16, jnp.concatenate([hi, lo], axis=1))
        g = g2[:, :LANES] + g2[:, LANES:]
        prepped.append((q, 1.0 - f, lf, g, v_ref[rows, :].astype(BF16)))

    staged = []
    for q, k, lf, g, vb in prepped:
        eg = jnp.exp(g)
        total = eg[0:1] if rev else eg[c - 1:c]
        g_tot = g[0:1] if rev else g[c - 1:c]
        q_inter = (q * eg).astype(BF16)
        d_s = _dot(vb, (k * jnp.exp(g_tot - g)).astype(BF16), _TN)
        g_excl = g - lf
        scores = []
        for i in range(c // SUB):
            r0 = i * SUB
            first = r0 + SUB - 1 if rev else r0
            g_ref = g_excl[first:first + 1]
            q_t = (q[r0:r0 + SUB] * jnp.exp(g[r0:r0 + SUB] - g_ref)).astype(BF16)
            k_t = (k * jnp.exp(jnp.minimum(g_ref - g, GROW_CAP))).astype(BF16)
            scores.append(_dot(q_t, k_t, _NT))
        staged.append((q_inter, d_s, total, jnp.concatenate(scores, axis=0), vb))

    intra = [_dot((sc * allowed).astype(BF16), vb) for _, _, _, sc, vb in staged]

    outs = []
    s_t = s_ref[...]
    for (q_inter, d_s, total, _, _), o_intra in zip(staged, intra):
        outs.append(_dot(q_inter, s_t.astype(BF16), _NT) + o_intra)
        s_t = total * s_t + d_s
    s_ref[...] = s_t
    return outs


def _hgrn_kernel(lb_ref, tri_f_ref, tri_b_ref, qf_ref, ff_ref, vf_ref, qb_ref, fb_ref, vb_ref,
                 o_ref, sf_ref, sb_ref, *, tt):
    n = pl.program_id(2)
    per_block = tt // CHUNK
    n_chunks = pl.num_programs(2) * per_block

    @pl.when(n == 0)
    def _():
        sf_ref[...] = jnp.zeros_like(sf_ref)
        sb_ref[...] = jnp.zeros_like(sb_ref)
        o_ref[...] = jnp.zeros_like(o_ref)

    lb = lb_ref[...]

    def smallest_gate(f_ref):
        return jnp.min(lb + (1.0 - lb) * _sigmoid(jnp.min(f_ref[...], axis=0, keepdims=True)))

    fast_ok = jnp.minimum(smallest_gate(ff_ref), smallest_gate(fb_ref)) > FAST_MIN_GATE

    def out_rows(chunk):
        return pl.ds(pl.multiple_of(chunk * CHUNK, CHUNK), CHUNK)

    @pl.when(fast_ok)
    def _():
        span = min(FAST_SPAN, per_block)

        def body(r, carry):
            ids = [r * span + j for j in range(span)]
            o_f = _hgrn_fast_span(lb, tri_f_ref[...], tri_f_ref[...], qf_ref, ff_ref, vf_ref, sf_ref, ids, False)
            o_b = _hgrn_fast_span(lb, tri_b_ref[...], tri_b_ref[...], qb_ref, fb_ref, vb_ref, sb_ref,
                                  [per_block - 1 - i for i in ids], True)
            for j in range(span):
                it = n * per_block + ids[j]
                o_ref[out_rows(it), :] += o_f[j]
                o_ref[out_rows(n_chunks - 1 - it), :] += o_b[j]
            return carry

        lax.fori_loop(0, per_block // span, body, 0)

    @pl.when(jnp.logical_not(fast_ok))
    def _():
        def body(c, carry):
            it = n * per_block + c
            o_f, s_f = _hgrn_chunk(qf_ref[out_rows(c), :], ff_ref[out_rows(c), :], vf_ref[out_rows(c), :], lb,
                                   sf_ref[...], tri_f_ref[...], False)
            sf_ref[...] = s_f
            o_ref[out_rows(it), :] += o_f
            cb = per_block - 1 - c
            o_b, s_b = _hgrn_chunk(qb_ref[out_rows(cb), :], fb_ref[out_rows(cb), :], vb_ref[out_rows(cb), :], lb,
                                   sb_ref[...], tri_b_ref[...], True)
            sb_ref[...] = s_b
            o_ref[out_rows(n_chunks - 1 - it), :] += o_b
            return carry

        lax.fori_loop(0, per_block, body, 0)


def _hgrn2(proj, lb):
    b, l, _ = proj.shape
    tt = min(l, TIME_BLOCK)
    nb = l // tt
    ones = np.ones((CHUNK, CHUNK), np.float32)
    tri_f, tri_b = jnp.asarray(np.tril(ones)), jnp.asarray(np.triu(ones))

    def col(off, bwd):
        base = off // LANES
        if bwd:
            return pl.BlockSpec((None, tt, LANES), lambda i, h, n: (i, nb - 1 - n, base + h))
        return pl.BlockSpec((None, tt, LANES), lambda i, h, n: (i, n, base + h))

    sq_spec = pl.BlockSpec((CHUNK, CHUNK), lambda i, h, n: (0, 0))
    return pl.pallas_call(
        functools.partial(_hgrn_kernel, tt=tt),
        grid=(b, HG_HEADS, nb),
        in_specs=[pl.BlockSpec((1, LANES), lambda i, h, n: (0, h)), sq_spec, sq_spec,
                  col(OFF_HQ, False), col(OFF_HF_FWD, False), col(OFF_HI, False),
                  col(OFF_HQ, True), col(OFF_HF_BWD, True), col(OFF_HI, True)],
        out_specs=pl.BlockSpec((None, l, HG_DV), lambda i, h, n: (i, 0, h)),
        out_shape=jax.ShapeDtypeStruct((b, l, HG_HEADS * HG_DV), F32),
        scratch_shapes=[pltpu.VMEM((HG_DV, HG_DK), F32), pltpu.VMEM((HG_DV, HG_DK), F32)],
        compiler_params=pltpu.CompilerParams(dimension_semantics=("parallel", "parallel", "arbitrary"),
                                             vmem_limit_bytes=VMEM_LIMIT),
        name="hgrn2",
    )(lb, tri_f, tri_b, proj, proj, proj, proj, proj, proj)


def _rotate(t, cos, sin):
    half = t.shape[1] // 2
    t1, t2 = t[:, :half], t[:, half:]
    return jnp.concatenate([t1 * cos - t2 * sin, t1 * sin + t2 * cos], axis=1)


def _ret_span(q_ref, k_ref, v_ref, cos_ref, sin_ref, r_ref, dmat, cross, kdec, cdec, chunk_ids):
    c = RET_CHUNK
    staged = []
    for cid in chunk_ids:
        rows = pl.ds(pl.multiple_of(cid * c, c), c)
        cos, sin = cos_ref[rows, :], sin_ref[rows, :]
        qr = _rotate(q_ref[rows, :], cos, sin).astype(BF16)
        kr = _rotate(k_ref[rows, :], cos, sin) * (RET_DK ** -0.5)
        vb = v_ref[rows, :].astype(BF16)
        inner = _dot(qr, kr.astype(BF16), _NT)
        d_r = _dot((kr * kdec).astype(BF16), vb, _TN)
        staged.append((qr, vb, inner, d_r))
    intra = [_dot((inner * dmat).astype(BF16), vb) for _, vb, inner, _ in staged]
    outs = []
    r = r_ref[...]
    for (qr, _, _, d_r), o_intra in zip(staged, intra):
        outs.append(o_intra + cross * _dot(qr, r.astype(BF16)))
        r = cdec * r + d_r
    r_ref[...] = r
    return outs


def _ret_kernel(dm_f_ref, dm_b_ref, cr_f_ref, cr_b_ref, kd_f_ref, kd_b_ref, cd_ref,
                qf_ref, kf_ref, vf_ref, cf_ref, sf_ref, qb_ref, kb_ref, vb_ref, cb_ref, sb_ref,
                o_ref, rf_ref, rb_ref, *, tt):
    n = pl.program_id(2)
    per_block = tt // RET_CHUNK
    n_chunks = pl.num_programs(2) * per_block
    span = min(RET_SPAN, per_block)

    @pl.when(n == 0)
    def _():
        rf_ref[...] = jnp.zeros_like(rf_ref)
        rb_ref[...] = jnp.zeros_like(rb_ref)
        o_ref[...] = jnp.zeros_like(o_ref)

    cdec = cd_ref[...]

    def out_rows(chunk):
        return pl.ds(pl.multiple_of(chunk * RET_CHUNK, RET_CHUNK), RET_CHUNK)

    def body(r, carry):
        ids = [r * span + j for j in range(span)]
        o_f = _ret_span(qf_ref, kf_ref, vf_ref, cf_ref, sf_ref, rf_ref, dm_f_ref[...], cr_f_ref[...],
                        kd_f_ref[...], cdec, ids)
        o_b = _ret_span(qb_ref, kb_ref, vb_ref, cb_ref, sb_ref, rb_ref, dm_b_ref[...], cr_b_ref[...],
                        kd_b_ref[...], cdec, [per_block - 1 - i for i in ids])
        for j in range(span):
            it = n * per_block + ids[j]
            o_ref[out_rows(it), :] += o_f[j]
            o_ref[out_rows(n_chunks - 1 - it), :] += o_b[j]
        return carry

    lax.fori_loop(0, per_block // span, body, 0)


def _retention_consts():
    idx = np.arange(RET_CHUNK, dtype=np.float64)
    lg = np.log(1.0 - 2.0 ** (-5.0 - np.arange(RET_HEADS, dtype=np.float64)))[:, None, None]
    rel = idx[:, None] - idx[None, :]
    dm_f = np.where(rel >= 0, np.exp(np.maximum(rel, 0.0) * lg), 0.0)
    dm_b = np.transpose(dm_f, (0, 2, 1))
    wide = np.ones((1, 1, RET_DK))
    cr_f = np.exp((idx + 1.0)[None, :, None] * lg) * wide
    kd_f = np.exp((RET_CHUNK - 1.0 - idx)[None, :, None] * lg) * wide
    cd = np.exp(RET_CHUNK * lg) * wide
    as32 = lambda a: jnp.asarray(a.astype(np.float32))
    return (as32(dm_f), as32(dm_b), as32(cr_f), as32(cr_f[:, ::-1]), as32(kd_f), as32(kd_f[:, ::-1]), as32(cd))


def _retention(proj, cos, sin):
    b, l, _ = proj.shape
    tt = min(l, TIME_BLOCK)
    nb = l // tt
    consts = _retention_consts()

    def tpos(n, bwd):
        return nb - 1 - n if bwd else n

    def col(off, bwd):
        base = off // RET_DK
        return pl.BlockSpec((None, tt, RET_DK), lambda i, h, n: (i, tpos(n, bwd), base + h))

    def tab(bwd):
        return pl.BlockSpec((None, tt, RET_DK // 2), lambda i, h, n: (i, tpos(n, bwd), 0))

    def per_head(shape):
        return pl.BlockSpec((None,) + shape, lambda i, h, n: (h, 0, 0))

    sq, wide, row = (RET_CHUNK, RET_CHUNK), (RET_CHUNK, RET_DK), (1, RET_DK)
    return pl.pallas_call(
        functools.partial(_ret_kernel, tt=tt),
        grid=(b, RET_HEADS, nb),
        in_specs=[per_head(sq), per_head(sq), per_head(wide), per_head(wide), per_head(wide), per_head(wide),
                  per_head(row),
                  col(OFF_RQ, False), col(OFF_RK, False), col(OFF_RV, False), tab(False), tab(False),
                  col(OFF_RQ, True), col(OFF_RK, True), col(OFF_RV, True), tab(True), tab(True)],
        out_specs=pl.BlockSpec((None, l, RET_DV), lambda i, h, n: (i, 0, h)),
        out_shape=jax.ShapeDtypeStruct((b, l, RET_HEADS * RET_DV), F32),
        scratch_shapes=[pltpu.VMEM((RET_DK, RET_DV), F32), pltpu.VMEM((RET_DK, RET_DV), F32)],
        compiler_params=pltpu.CompilerParams(dimension_semantics=("parallel", "parallel", "arbitrary"),
                                             vmem_limit_bytes=VMEM_LIMIT),
        name="retention",
    )(*consts, proj, proj, proj, cos, sin, proj, proj, proj, cos, sin)


def _norm_heads(o, n_heads):
    d = o.shape[1] // n_heads
    parts = []
    for h in range(n_heads):
        oh = o[:, h * d:(h + 1) * d]
        parts.append(oh * lax.rsqrt(jnp.mean(oh * oh, -1, keepdims=True) + RMS_EPS))
    return jnp.concatenate(parts, axis=1)


def _mix_kernel(ohg_ref, hgate_ref, oret_ref, rgate_ref, ga_ref, gb_ref, x_ref, ghg_ref, gret_ref,
                wbh_ref, wbr_ref, wo_ref, lng_ref, lnb_ref, wr_ref, x1_ref, logit_ref):
    a = _norm_heads(ohg_ref[...], HG_HEADS) * ghg_ref[...] * _silu(hgate_ref[...])
    y_hg = _dot(a.astype(BF16), wbh_ref[...])
    c = _norm_heads(oret_ref[...], RET_HEADS) * gret_ref[...] * _silu(rgate_ref[...])
    y_ret = _dot(c.astype(BF16), wbr_ref[...])
    merged = _sigmoid(ga_ref[...]) * y_hg + _sigmoid(gb_ref[...]) * y_ret
    mix = _dot(merged.astype(BF16), wo_ref[...])
    x1 = _layer_norm(DEEPNORM_ALPHA * x_ref[...] + mix, lng_ref[...], lnb_ref[...])
    x1_ref[...] = x1
    x_hi = x1.astype(BF16)
    x_lo = (x1 - x_hi.astype(F32)).astype(BF16)
    wr = wr_ref[...]
    by_hi = _dot(x_hi, wr)
    logit_ref[...] = by_hi[:, :LANES] + by_hi[:, LANES:] + _dot(x_lo, wr[:, :LANES])


def _mixer_out(o_hg, o_ret, proj, x, g_hg, g_ret, wbh, wbr, wo, ln_g, ln_b, w_route):
    t = x.shape[0]
    tm = min(t, MIX_TILE)
    d = D_MODEL
    rows = lambda cb: pl.BlockSpec((tm, d), lambda i: (i, cb))
    full = lambda shape: pl.BlockSpec(shape, lambda i: (0, 0))
    return pl.pallas_call(
        _mix_kernel,
        grid=(t // tm,),
        in_specs=[rows(0), rows(OFF_HGATE // d), rows(0), rows(OFF_RGATE // d), rows(OFF_GA // d),
                  rows(OFF_GB // d), rows(0), full((1, d)), full((1, d)),
                  full((d, d)), full((d, d)), full((d, d)), full((1, d)), full((1, d)), full((d, 2 * LANES))],
        out_specs=[pl.BlockSpec((tm, d), lambda i: (i, 0)), pl.BlockSpec((tm, LANES), lambda i: (i, 0))],
        out_shape=[jax.ShapeDtypeStruct((t, d), F32), jax.ShapeDtypeStruct((t, LANES), F32)],
        compiler_params=pltpu.CompilerParams(dimension_semantics=("parallel",), vmem_limit_bytes=VMEM_LIMIT),
        name="mixer_out",
    )(o_hg, proj, o_ret, proj, proj, proj, x, g_hg, g_ret, wbh, wbr, wo, ln_g, ln_b, w_route)


GROUP_LANE0 = N_EXPERTS


def _route_kernel(logit_ref, bias_ref, tri_ref, idx_ref, wt_ref, cnt_ref, carry_ref):
    i = pl.program_id(0)

    @pl.when(i == 0)
    def _():
        carry_ref[...] = jnp.zeros_like(carry_ref)

    lg = logit_ref[...] + bias_ref[...]
    tm = lg.shape[0]
    lane = lax.broadcasted_iota(jnp.int32, (tm, LANES), 1)
    first_of = lambda hit: jnp.min(jnp.where(hit, lane, LANES), -1, keepdims=True)

    g_mask = (lane >= GROUP_LANE0) & (lane < GROUP_LANE0 + N_GROUPS)
    g_l = jnp.where(g_mask, lg, NEG_BIG)
    g_max = jnp.max(g_l, -1, keepdims=True)
    grp = first_of(g_l == g_max) - GROUP_LANE0
    p_grp = 1.0 / jnp.sum(jnp.where(g_mask, jnp.exp(g_l - g_max), 0.0), -1, keepdims=True)

    e_lo = grp * EXPERTS_PER_GROUP
    e_l = jnp.where((lane >= e_lo) & (lane < e_lo + EXPERTS_PER_GROUP), lg, NEG_BIG)
    m1 = jnp.max(e_l, -1, keepdims=True)
    i1 = first_of(e_l == m1)
    e_l2 = jnp.where(lane == i1, NEG_BIG, e_l)
    m2 = jnp.max(e_l2, -1, keepdims=True)
    i2 = first_of(e_l2 == m2)
    r = jnp.exp(m2 - m1)
    w1 = p_grp / (1.0 + r)
    w2 = p_grp * r / (1.0 + r)

    hit1 = (lane == i1).astype(F32)
    hit2 = (lane == i2).astype(F32)
    both = hit1 + hit2
    before = _dot(tri_ref[...], both.astype(BF16)) + carry_ref[...]
    r1 = jnp.sum(hit1 * before, -1, keepdims=True).astype(jnp.int32)
    r2 = jnp.sum(hit2 * before, -1, keepdims=True).astype(jnp.int32)
    carry_ref[...] += jnp.sum(both, 0, keepdims=True)
    cnt_ref[...] = carry_ref[...]

    slot = lax.broadcasted_iota(jnp.int32, (tm, 8), 1)
    idx_ref[...] = jnp.where(slot == 0, i1, jnp.where(slot == 1, i2, jnp.where(slot == 2, r1, r2)))
    wt_ref[...] = jnp.where(slot == 0, w1, jnp.where(slot == 1, w2, 0.0))


def _route(logits, bias):
    t = logits.shape[0]
    tm = min(t, ROUTE_TILE)
    tri = jnp.asarray(np.tril(np.ones((tm, tm), np.float32), -1)).astype(BF16)
    return pl.pallas_call(
        _route_kernel,
        grid=(t // tm,),
        in_specs=[pl.BlockSpec((tm, LANES), lambda i: (i, 0)), pl.BlockSpec((1, LANES), lambda i: (0, 0)),
                  pl.BlockSpec((tm, tm), lambda i: (0, 0))],
        out_specs=[pl.BlockSpec((tm, 8), lambda i: (i, 0)), pl.BlockSpec((tm, 8), lambda i: (i, 0)),
                   pl.BlockSpec((1, LANES), lambda i: (0, 0))],
        out_shape=[jax.ShapeDtypeStruct((t, 8), jnp.int32), jax.ShapeDtypeStruct((t, 8), F32),
                   jax.ShapeDtypeStruct((1, LANES), F32)],
        scratch_shapes=[pltpu.VMEM((1, LANES), F32)],
        compiler_params=pltpu.CompilerParams(dimension_semantics=("arbitrary",)),
        name="route",
    )(logits, bias, tri)


def _row_copy(src, row, dst, dst_row, sem):
    return pltpu.make_async_copy(src.at[pl.ds(row, 1)], dst.at[pl.ds(dst_row, 1)], sem)


def _expert_kernel(be_ref, rows_ref, nused_ref, x_hbm, w1_ref, w3_ref, w2_ref, y_ref, buf, sem):
    i = pl.program_id(0)
    n_used = nused_ref[0]
    blk = buf.shape[1]

    def start(block, slot):
        def body(r, carry):
            _row_copy(x_hbm, rows_ref[block * blk + r], buf.at[slot], r, sem.at[slot]).start()
            return carry
        lax.fori_loop(0, blk, body, 0, unroll=DMA_UNROLL)

    @pl.when(i == 0)
    def _():
        start(0, 0)

    @pl.when(i + 1 < n_used)
    def _():
        start(i + 1, (i + 1) % 2)

    @pl.when(i < n_used)
    def _():
        slot = i % 2
        pltpu.make_async_copy(x_hbm.at[pl.ds(0, blk)], buf.at[slot], sem.at[slot]).wait()
        xs = buf[slot].astype(BF16)
        h = _silu(_dot(xs, w1_ref[...])) * _dot(xs, w3_ref[...])
        y_ref[...] = _dot(h.astype(BF16), w2_ref[...])

    @pl.when(i >= n_used)
    def _():
        y_ref[...] = jnp.zeros_like(y_ref)


def _experts(x1, block_expert, rows, n_used, w1, w3, w2, blk):
    n_blocks = block_expert.shape[0]
    d, de = D_MODEL, D_EXPERT
    grid_spec = pltpu.PrefetchScalarGridSpec(
        num_scalar_prefetch=3,
        grid=(n_blocks,),
        in_specs=[pl.BlockSpec(memory_space=pl.ANY),
                  pl.BlockSpec((None, d, de), lambda i, be, rw, nu: (be[i], 0, 0)),
                  pl.BlockSpec((None, d, de), lambda i, be, rw, nu: (be[i], 0, 0)),
                  pl.BlockSpec((None, de, d), lambda i, be, rw, nu: (be[i], 0, 0))],
        out_specs=pl.BlockSpec((blk, d), lambda i, be, rw, nu: (i, 0)),
        scratch_shapes=[pltpu.VMEM((2, blk, d), F32), pltpu.SemaphoreType.DMA((2,))],
    )
    return pl.pallas_call(
        _expert_kernel,
        grid_spec=grid_spec,
        out_shape=jax.ShapeDtypeStruct((n_blocks * blk, d), F32),
        compiler_params=pltpu.CompilerParams(dimension_semantics=("arbitrary",), vmem_limit_bytes=VMEM_LIMIT),
        name="experts",
    )(block_expert, rows, n_used, x1, w1, w3, w2)


def _combine_kernel(dest_ref, y_hbm, x1_ref, wt_ref, lng_ref, lnb_ref, o_ref, buf, sem):
    i = pl.program_id(0)
    n = pl.num_programs(0)
    tm = x1_ref.shape[0]

    def start(tile, slot):
        def body(r, carry):
            for k in range(TOP_K):
                _row_copy(y_hbm, dest_ref[(tile * tm + r) * TOP_K + k], buf.at[slot, k], r, sem.at[slot]).start()
            return carry
        lax.fori_loop(0, tm, body, 0, unroll=DMA_UNROLL)

    def drain(slot):
        for k in range(TOP_K):
            pltpu.make_async_copy(y_hbm.at[pl.ds(0, tm)], buf.at[slot, k], sem.at[slot]).wait()

    @pl.when(i == 0)
    def _():
        start(0, 0)

    @pl.when(i + 1 < n)
    def _():
        start(i + 1, (i + 1) % 2)

    slot = i % 2
    drain(slot)
    wt = wt_ref[...]
    moe = wt[:, 0:1] * buf[slot, 0] + wt[:, 1:2] * buf[slot, 1]
    o_ref[...] = _layer_norm(DEEPNORM_ALPHA * x1_ref[...] + moe, lng_ref[...], lnb_ref[...])


def _combine(dest, y, x1, wts, ln_g, ln_b):
    t, d = x1.shape
    tm = min(t, ROUTE_TILE)
    grid_spec = pltpu.PrefetchScalarGridSpec(
        num_scalar_prefetch=1,
        grid=(t // tm,),
        in_specs=[pl.BlockSpec(memory_space=pl.ANY),
                  pl.BlockSpec((tm, d), lambda i, ds: (i, 0)),
                  pl.BlockSpec((tm, 8), lambda i, ds: (i, 0)),
                  pl.BlockSpec((1, d), lambda i, ds: (0, 0)),
                  pl.BlockSpec((1, d), lambda i, ds: (0, 0))],
        out_specs=pl.BlockSpec((tm, d), lambda i, ds: (i, 0)),
        scratch_shapes=[pltpu.VMEM((2, TOP_K, tm, d), F32), pltpu.SemaphoreType.DMA((2,))],
    )
    return pl.pallas_call(
        _combine_kernel,
        grid_spec=grid_spec,
        out_shape=jax.ShapeDtypeStruct((t, d), F32),
        compiler_params=pltpu.CompilerParams(dimension_semantics=("arbitrary",), vmem_limit_bytes=VMEM_LIMIT),
        name="combine",
    )(dest, y, x1, wts, ln_g, ln_b)


def _deinterleave_cols(w):
    dk = RET_DK
    wh = w.reshape(w.shape[0], RET_HEADS, dk // 2, 2)
    return jnp.concatenate([wh[..., 0], wh[..., 1]], axis=-1).reshape(w.shape[0], RET_HEADS * dk)


def kernel(x, positions, w_in, hg_lb_logits, hg_norm_g, ret_norm_g, w_branch_hg, w_branch_ret, w_out,
           ln1_g, ln1_b, w_group, b_group, w_router, b_router, w1, w3, w2, ln2_g, ln2_b):
    b, l, d = x.shape
    t = b * l
    lb_cum = jnp.cumsum(jax.nn.softmax(hg_lb_logits.astype(F32), axis=0), axis=0)
    cos, sin = _rope_tables(positions)
    xf = x.reshape(t, d)
    for layer in range(DEPTH):
        lb = (lb_cum[layer + 1] - lb_cum[0]).reshape(1, -1)
        w = w_in[layer]
        w = jnp.concatenate([w[:, :OFF_RQ], _deinterleave_cols(w[:, OFF_RQ:OFF_RK]),
                             _deinterleave_cols(w[:, OFF_RK:OFF_RV]), w[:, OFF_RV:]], axis=1)
        proj = _matmul(xf.astype(BF16), w.astype(BF16), F32, 1024, 1024, "in_proj")
        proj3 = proj.reshape(b, l, PROJ_DIM)
        o_hg = _hgrn2(proj3, lb).reshape(t, d)
        o_ret = _retention(proj3, cos, sin).reshape(t, d)

        w_route = jnp.zeros((d, LANES), F32)
        w_route = w_route.at[:, :N_EXPERTS].set(w_router[layer]).at[:, GROUP_LANE0:GROUP_LANE0 + N_GROUPS].set(w_group[layer])
        b_route = jnp.zeros((1, LANES), F32)
        b_route = b_route.at[0, :N_EXPERTS].set(b_router[layer]).at[0, GROUP_LANE0:GROUP_LANE0 + N_GROUPS].set(b_group[layer])
        w_route_hi = w_route.astype(BF16)
        w_route = jnp.concatenate([w_route_hi, (w_route - w_route_hi.astype(F32)).astype(BF16)], axis=1)
        row = lambda v: v.reshape(1, -1)
        x1, logits = _mixer_out(o_hg, o_ret, proj, xf, row(hg_norm_g[layer]), row(ret_norm_g[layer]),
                                w_branch_hg[layer].astype(BF16), w_branch_ret[layer].astype(BF16),
                                w_out[layer].astype(BF16), row(ln1_g[layer]), row(ln1_b[layer]), w_route)

        idx, wts, counts = _route(logits, b_route)
        blk = MOE_BLOCK
        counts = counts[0, :N_EXPERTS].astype(jnp.int32)
        padded = (counts + blk - 1) // blk * blk
        pend = jnp.cumsum(padded)
        pstart = pend - padded
        dest = pstart[idx[:, :TOP_K]] + idx[:, TOP_K:2 * TOP_K]
        n_blocks = t * TOP_K // blk + N_EXPERTS
        block_row0 = jnp.arange(n_blocks, dtype=jnp.int32) * blk
        block_expert = jnp.minimum(jnp.sum((pend[None, :] <= block_row0[:, None]).astype(jnp.int32), axis=1),
                                   N_EXPERTS - 1)
        n_used = (pend[-1:] // blk).astype(jnp.int32)
        dest = dest.reshape(-1).astype(jnp.int32)
        tok = jnp.arange(t * TOP_K, dtype=jnp.int32) // TOP_K
        rows = jnp.zeros((n_blocks * blk,), jnp.int32).at[dest].set(tok)
        y = _experts(x1, block_expert, rows, n_used, w1[layer].astype(BF16), w3[layer].astype(BF16),
                     w2[layer].astype(BF16), blk)
        xf = _combine(dest, y, x1, wts, row(ln2_g[layer]), row(ln2_b[layer]))
    return xf.reshape(b, l, d)
```

```python
import functools

import numpy as np
import jax
import jax.numpy as jnp
from jax import lax
from jax.experimental import pallas as pl
from jax.experimental.pallas import tpu as pltpu

D_MODEL = 1024
HG_HEADS = 8
HG_DK = 128
HG_DV = D_MODEL // HG_HEADS
RET_HEADS = 4
RET_DK = 256
RET_DV = 256
ROPE_BASE = 10000.0
OFF_HQ = 0
OFF_HF_FWD = 1024
OFF_HF_BWD = 2048
OFF_HI = 3072
OFF_HGATE = 4096
OFF_RQ = 5120
OFF_RK = 6144
OFF_RV = 7168
OFF_RGATE = 8192
OFF_GA = 9216
OFF_GB = 10240
PROJ_DIM = 11264
N_GROUPS = 4
EXPERTS_PER_GROUP = 8
N_EXPERTS = N_GROUPS * EXPERTS_PER_GROUP
TOP_K = 2
D_EXPERT = 512
DEPTH = 1
DEEPNORM_ALPHA = (2 * DEPTH) ** 0.25
LN_EPS = 1e-5
RMS_EPS = 1e-6

LANES = 128
CHUNK = 64
RET_CHUNK = 128
RET_SPAN = 4
SUB = 16
TIME_BLOCK = 512
MOE_BLOCK = 256
ROUTE_TILE = 256
DMA_UNROLL = 16
MIX_TILE = 256
VMEM_LIMIT = 56 * 1024 * 1024
NEG_BIG = -1e30
FAST_MIN_GATE = float(np.exp(-60.0 / SUB))
GROW_CAP = 64.0
FAST_SPAN = 4

F32 = jnp.float32
BF16 = jnp.bfloat16
_NT = (((1,), (1,)), ((), ()))
_TN = (((0,), (0,)), ((), ()))


def _sigmoid(v):
    return 1.0 / (1.0 + jnp.exp(-v))


def _silu(v):
    return v * _sigmoid(v)


def _dot(a, b, dims=None):
    if dims is None:
        return jnp.dot(a, b, preferred_element_type=F32)
    return lax.dot_general(a, b, dims, preferred_element_type=F32)


def _layer_norm(z, g, b):
    mu = jnp.mean(z, -1, keepdims=True)
    zc = z - mu
    var = jnp.mean(zc * zc, -1, keepdims=True)
    return zc * lax.rsqrt(var + LN_EPS) * g + b


def _rope_kernel(pos_ref, inv_ref, cos_ref, sin_ref):
    ang = pos_ref[...] * inv_ref[...]
    cos_ref[...] = jnp.cos(ang)
    sin_ref[...] = jnp.sin(ang)


def _rope_tables(positions):
    b, l = positions.shape
    tt = min(l, 1024)
    half = RET_DK // 2
    inv = (1.0 / (ROPE_BASE ** jnp.linspace(0.0, 1.0, half, dtype=F32))).reshape(1, half)
    pos = positions.astype(F32).reshape(b, l, 1)
    spec = pl.BlockSpec((None, tt, half), lambda i, j: (i, j, 0))
    return pl.pallas_call(
        _rope_kernel,
        grid=(b, l // tt),
        in_specs=[pl.BlockSpec((None, tt, 1), lambda i, j: (i, j, 0)),
                  pl.BlockSpec((1, half), lambda i, j: (0, 0))],
        out_specs=[spec, spec],
        out_shape=[jax.ShapeDtypeStruct((b, l, half), F32)] * 2,
        compiler_params=pltpu.CompilerParams(dimension_semantics=("parallel", "parallel")),
        name="rope_tables",
    )(pos, inv)


def _matmul_kernel(x_ref, w_ref, o_ref):
    o_ref[...] = _dot(x_ref[...], w_ref[...]).astype(o_ref.dtype)


def _matmul(x, w, out_dtype, tm, tn, name):
    m, k = x.shape
    n = w.shape[1]
    tm, tn = min(tm, m), min(tn, n)
    return pl.pallas_call(
        _matmul_kernel,
        grid=(n // tn, m // tm),
        in_specs=[pl.BlockSpec((tm, k), lambda j, i: (i, 0)),
                  pl.BlockSpec((k, tn), lambda j, i: (0, j))],
        out_specs=pl.BlockSpec((tm, tn), lambda j, i: (i, j)),
        out_shape=jax.ShapeDtypeStruct((m, n), out_dtype),
        compiler_params=pltpu.CompilerParams(dimension_semantics=("parallel", "parallel"),
                                             vmem_limit_bytes=VMEM_LIMIT),
        name=name,
    )(x, w)


def _hgrn_chunk(hq, hf, v, lb, s_t, tri, rev):
    c = hq.shape[0]
    q = _silu(hq)
    f = lb + (1.0 - lb) * _sigmoid(hf)
    k = 1.0 - f
    lf = jnp.log(f)
    g = jnp.dot(tri, lf, precision=lax.Precision.HIGHEST, preferred_element_type=F32)
    g_excl = g - lf
    g_tot = g[0:1] if rev else g[c - 1:c]

    vb = v.astype(BF16)
    o_inter = _dot((q * jnp.exp(g)).astype(BF16), s_t.astype(BF16), _NT)
    k_dec = (k * jnp.exp(g_tot - g)).astype(BF16)
    s_new = jnp.exp(g_tot) * s_t + _dot(vb, k_dec, _TN)

    ones = jnp.ones((LANES, LANES), BF16)
    t_idx = lax.broadcasted_iota(jnp.int32, (SUB, LANES), 0)
    outs = []
    for i in range(c // SUB):
        r0 = i * SUB
        q_i, k_i, g_i, v_i = q[r0:r0 + SUB], k[r0:r0 + SUB], g[r0:r0 + SUB], v[r0:r0 + SUB]
        lo, hi = (r0 + SUB, c) if rev else (0, r0)
        acc = jnp.zeros((SUB, v.shape[1]), F32)
        if hi > lo:
            first = r0 + SUB - 1 if rev else r0
            g_ref = g_excl[first:first + 1]
            q_t = (q_i * jnp.exp(g_i - g_ref)).astype(BF16)
            k_t = (k[lo:hi] * jnp.exp(g_ref - g[lo:hi])).astype(BF16)
            a = _dot(q_t, k_t, _NT)
            acc = _dot(a.astype(BF16), vb[lo:hi])
        prods = []
        for j in range(SUB):
            valid = (t_idx <= j) if rev else (t_idx >= j)
            e = jnp.exp(jnp.where(valid, g_i - g_i[j:j + 1], NEG_BIG))
            prods.append(q_i * k_i[j:j + 1] * e)
        a_rep = _dot(jnp.concatenate(prods, axis=0).astype(BF16), ones)
        for j in range(SUB):
            acc = acc + a_rep[j * SUB:(j + 1) * SUB] * v_i[j:j + 1]
        outs.append(acc)
    return o_inter + jnp.concatenate(outs, axis=0), s_new


def _hgrn_fast_span(lb, tri, allowed, q_ref, f_ref, v_ref, s_ref, chunk_ids, rev):
    c = CHUNK
    tri_b16 = tri.astype(BF16)
    prepped = []
    for cid in chunk_ids:
        rows = pl.ds(pl.multiple_of(cid * c, c), c)
        q = _silu(q_ref[rows, :])
        f = lb + (1.0 - lb) * _sigmoid(f_ref[rows, :])
        lf = jnp.log(f)
        hi = lf.astype(BF16)
        lo = (lf - hi.astype(F32)).astype(BF16)
        g2 = _dot(tri_b16, jnp.concatenate([hi, lo], axis=1))
        g = g2[:, :LANES] + g2[:, LANES:]
        prepped.append((q, 1.0 - f, lf, g, v_ref[rows, :].astype(BF16)))

    staged = []
    for q, k, lf, g, vb in prepped:
        eg = jnp.exp(g)
        total = eg[0:1] if rev else eg[c - 1:c]
        g_tot = g[0:1] if rev else g[c - 1:c]
        q_inter = (q * eg).astype(BF16)
        d_s = _dot(vb, (k * jnp.exp(g_tot - g)).astype(BF16), _TN)
        g_excl = g - lf
        scores = []
        for i in range(c // SUB):
            r0 = i * SUB
            first = r0 + SUB - 1 if rev else r0
            g_ref = g_excl[first:first + 1]
            q_t = (q[r0:r0 + SUB] * jnp.exp(g[r0:r0 + SUB] - g_ref)).astype(BF16)
            k_t = (k * jnp.exp(jnp.minimum(g_ref - g, GROW_CAP))).astype(BF16)
            scores.append(_dot(q_t, k_t, _NT))
        staged.append((q_inter, d_s, total, jnp.concatenate(scores, axis=0), vb))

    intra = [_dot((sc * allowed).astype(BF16), vb) for _, _, _, sc, vb in staged]

    outs = []
    s_t = s_ref[...]
    for (q_inter, d_s, total, _, _), o_intra in zip(staged, intra):
        outs.append(_dot(q_inter, s_t.astype(BF16), _NT) + o_intra)
        s_t = total * s_t + d_s
    s_ref[...] = s_t
    return outs


def _hgrn_kernel(lb_ref, tri_f_ref, tri_b_ref, qf_ref, ff_ref, vf_ref, qb_ref, fb_ref, vb_ref,
                 o_ref, sf_ref, sb_ref, *, tt):
    n = pl.program_id(2)
    per_block = tt // CHUNK
    n_chunks = pl.num_programs(2) * per_block

    @pl.when(n == 0)
    def _():
        sf_ref[...] = jnp.zeros_like(sf_ref)
        sb_ref[...] = jnp.zeros_like(sb_ref)
        o_ref[...] = jnp.zeros_like(o_ref)

    lb = lb_ref[...]

    def smallest_gate(f_ref):
        return jnp.min(lb + (1.0 - lb) * _sigmoid(jnp.min(f_ref[...], axis=0, keepdims=True)))

    fast_ok = jnp.minimum(smallest_gate(ff_ref), smallest_gate(fb_ref)) > FAST_MIN_GATE

    def out_rows(chunk):
        return pl.ds(pl.multiple_of(chunk * CHUNK, CHUNK), CHUNK)

    @pl.when(fast_ok)
    def _():
        span = min(FAST_SPAN, per_block)

        def body(r, carry):
            ids = [r * span + j for j in range(span)]
            o_f = _hgrn_fast_span(lb, tri_f_ref[...], tri_f_ref[...], qf_ref, ff_ref, vf_ref, sf_ref, ids, False)
            o_b = _hgrn_fast_span(lb, tri_b_ref[...], tri_b_ref[...], qb_ref, fb_ref, vb_ref, sb_ref,
                                  [per_block - 1 - i for i in ids], True)
            for j in range(span):
                it = n * per_block + ids[j]
                o_ref[out_rows(it), :] += o_f[j]
                o_ref[out_rows(n_chunks - 1 - it), :] += o_b[j]
            return carry

        lax.fori_loop(0, per_block // span, body, 0)

    @pl.when(jnp.logical_not(fast_ok))
    def _():
        def body(c, carry):
            it = n * per_block + c
            o_f, s_f = _hgrn_chunk(qf_ref[out_rows(c), :], ff_ref[out_rows(c), :], vf_ref[out_rows(c), :], lb,
                                   sf_ref[...], tri_f_ref[...], False)
            sf_ref[...] = s_f
            o_ref[out_rows(it), :] += o_f
            cb = per_block - 1 - c
            o_b, s_b = _hgrn_chunk(qb_ref[out_rows(cb), :], fb_ref[out_rows(cb), :], vb_ref[out_rows(cb), :], lb,
                                   sb_ref[...], tri_b_ref[...], True)
            sb_ref[...] = s_b
            o_ref[out_rows(n_chunks - 1 - it), :] += o_b
            return carry

        lax.fori_loop(0, per_block, body, 0)


def _hgrn2(proj, lb):
    b, l, _ = proj.shape
    tt = min(l, TIME_BLOCK)
    nb = l // tt
    ones = np.ones((CHUNK, CHUNK), np.float32)
    tri_f, tri_b = jnp.asarray(np.tril(ones)), jnp.asarray(np.triu(ones))

    def col(off, bwd):
        base = off // LANES
        if bwd:
            return pl.BlockSpec((None, tt, LANES), lambda i, h, n: (i, nb - 1 - n, base + h))
        return pl.BlockSpec((None, tt, LANES), lambda i, h, n: (i, n, base + h))

    sq_spec = pl.BlockSpec((CHUNK, CHUNK), lambda i, h, n: (0, 0))
    return pl.pallas_call(
        functools.partial(_hgrn_kernel, tt=tt),
        grid=(b, HG_HEADS, nb),
        in_specs=[pl.BlockSpec((1, LANES), lambda i, h, n: (0, h)), sq_spec, sq_spec,
                  col(OFF_HQ, False), col(OFF_HF_FWD, False), col(OFF_HI, False),
                  col(OFF_HQ, True), col(OFF_HF_BWD, True), col(OFF_HI, True)],
        out_specs=pl.BlockSpec((None, l, HG_DV), lambda i, h, n: (i, 0, h)),
        out_shape=jax.ShapeDtypeStruct((b, l, HG_HEADS * HG_DV), F32),
        scratch_shapes=[pltpu.VMEM((HG_DV, HG_DK), F32), pltpu.VMEM((HG_DV, HG_DK), F32)],
        compiler_params=pltpu.CompilerParams(dimension_semantics=("parallel", "parallel", "arbitrary"),
                                             vmem_limit_bytes=VMEM_LIMIT),
        name="hgrn2",
    )(lb, tri_f, tri_b, proj, proj, proj, proj, proj, proj)


def _rotate(t, cos, sin):
    half = t.shape[1] // 2
    t1, t2 = t[:, :half], t[:, half:]
    return jnp.concatenate([t1 * cos - t2 * sin, t1 * sin + t2 * cos], axis=1)


def _ret_span(q_ref, k_ref, v_ref, cos_ref, sin_ref, r_ref, dmat, cross, kdec, cdec, chunk_ids):
    c = RET_CHUNK
    staged = []
    for cid in chunk_ids:
        rows = pl.ds(pl.multiple_of(cid * c, c), c)
        cos, sin = cos_ref[rows, :], sin_ref[rows, :]
        qr = _rotate(q_ref[rows, :], cos, sin).astype(BF16)
        kr = _rotate(k_ref[rows, :], cos, sin) * (RET_DK ** -0.5)
        vb = v_ref[rows, :].astype(BF16)
        inner = _dot(qr, kr.astype(BF16), _NT)
        d_r = _dot((kr * kdec).astype(BF16), vb, _TN)
        staged.append((qr, vb, inner, d_r))
    intra = [_dot((inner * dmat).astype(BF16), vb) for _, vb, inner, _ in staged]
    outs = []
    r = r_ref[...]
    for (qr, _, _, d_r), o_intra in zip(staged, intra):
        outs.append(o_intra + cross * _dot(qr, r.astype(BF16)))
        r = cdec * r + d_r
    r_ref[...] = r
    return outs


def _ret_kernel(dm_f_ref, dm_b_ref, cr_f_ref, cr_b_ref, kd_f_ref, kd_b_ref, cd_ref,
                qf_ref, kf_ref, vf_ref, cf_ref, sf_ref, qb_ref, kb_ref, vb_ref, cb_ref, sb_ref,
                o_ref, rf_ref, rb_ref, *, tt):
    n = pl.program_id(2)
    per_block = tt // RET_CHUNK
    n_chunks = pl.num_programs(2) * per_block
    span = min(RET_SPAN, per_block)

    @pl.when(n == 0)
    def _():
        rf_ref[...] = jnp.zeros_like(rf_ref)
        rb_ref[...] = jnp.zeros_like(rb_ref)
        o_ref[...] = jnp.zeros_like(o_ref)

    cdec = cd_ref[...]

    def out_rows(chunk):
        return pl.ds(pl.multiple_of(chunk * RET_CHUNK, RET_CHUNK), RET_CHUNK)

    def body(r, carry):
        ids = [r * span + j for j in range(span)]
        o_f = _ret_span(qf_ref, kf_ref, vf_ref, cf_ref, sf_ref, rf_ref, dm_f_ref[...], cr_f_ref[...],
                        kd_f_ref[...], cdec, ids)
        o_b = _ret_span(qb_ref, kb_ref, vb_ref, cb_ref, sb_ref, rb_ref, dm_b_ref[...], cr_b_ref[...],
                        kd_b_ref[...], cdec, [per_block - 1 - i for i in ids])
        for j in range(span):
            it = n * per_block + ids[j]
            o_ref[out_rows(it), :] += o_f[j]
            o_ref[out_rows(n_chunks - 1 - it), :] += o_b[j]
        return carry

    lax.fori_loop(0, per_block // span, body, 0)


def _retention_consts():
    idx = np.arange(RET_CHUNK, dtype=np.float64)
    lg = np.log(1.0 - 2.0 ** (-5.0 - np.arange(RET_HEADS, dtype=np.float64)))[:, None, None]
    rel = idx[:, None] - idx[None, :]
    dm_f = np.where(rel >= 0, np.exp(np.maximum(rel, 0.0) * lg), 0.0)
    dm_b = np.transpose(dm_f, (0, 2, 1))
    wide = np.ones((1, 1, RET_DK))
    cr_f = np.exp((idx + 1.0)[None, :, None] * lg) * wide
    kd_f = np.exp((RET_CHUNK - 1.0 - idx)[None, :, None] * lg) * wide
    cd = np.exp(RET_CHUNK * lg) * wide
    as32 = lambda a: jnp.asarray(a.astype(np.float32))
    return (as32(dm_f), as32(dm_b), as32(cr_f), as32(cr_f[:, ::-1]), as32(kd_f), as32(kd_f[:, ::-1]), as32(cd))


def _retention(proj, cos, sin):
    b, l, _ = proj.shape
    tt = min(l, TIME_BLOCK)
    nb = l // tt
    consts = _retention_consts()

    def tpos(n, bwd):
        return nb - 1 - n if bwd else n

    def col(off, bwd):
        base = off // RET_DK
        return pl.BlockSpec((None, tt, RET_DK), lambda i, h, n: (i, tpos(n, bwd), base + h))

    def tab(bwd):
        return pl.BlockSpec((None, tt, RET_DK // 2), lambda i, h, n: (i, tpos(n, bwd), 0))

    def per_head(shape):
        return pl.BlockSpec((None,) + shape, lambda i, h, n: (h, 0, 0))

    sq, wide, row = (RET_CHUNK, RET_CHUNK), (RET_CHUNK, RET_DK), (1, RET_DK)
    return pl.pallas_call(
        functools.partial(_ret_kernel, tt=tt),
        grid=(b, RET_HEADS, nb),
        in_specs=[per_head(sq), per_head(sq), per_head(wide), per_head(wide), per_head(wide), per_head(wide),
                  per_head(row),
                  col(OFF_RQ, False), col(OFF_RK, False), col(OFF_RV, False), tab(False), tab(False),
                  col(OFF_RQ, True), col(OFF_RK, True), col(OFF_RV, True), tab(True), tab(True)],
        out_specs=pl.BlockSpec((None, l, RET_DV), lambda i, h, n: (i, 0, h)),
        out_shape=jax.ShapeDtypeStruct((b, l, RET_HEADS * RET_DV), F32),
        scratch_shapes=[pltpu.VMEM((RET_DK, RET_DV), F32), pltpu.VMEM((RET_DK, RET_DV), F32)],
        compiler_params=pltpu.CompilerParams(dimension_semantics=("parallel", "parallel", "arbitrary"),
                                             vmem_limit_bytes=VMEM_LIMIT),
        name="retention",
    )(*consts, proj, proj, proj, cos, sin, proj, proj, proj, cos, sin)


def _norm_heads(o, n_heads):
    d = o.shape[1] // n_heads
    parts = []
    for h in range(n_heads):
        oh = o[:, h * d:(h + 1) * d]
        parts.append(oh * lax.rsqrt(jnp.mean(oh * oh, -1, keepdims=True) + RMS_EPS))
    return jnp.concatenate(parts, axis=1)


def _mix_kernel(ohg_ref, hgate_ref, oret_ref, rgate_ref, ga_ref, gb_ref, x_ref, ghg_ref, gret_ref,
                wbh_ref, wbr_ref, wo_ref, lng_ref, lnb_ref, wr_ref, x1_ref, logit_ref):
    a = _norm_heads(ohg_ref[...], HG_HEADS) * ghg_ref[...] * _silu(hgate_ref[...])
    y_hg = _dot(a.astype(BF16), wbh_ref[...])
    c = _norm_heads(oret_ref[...], RET_HEADS) * gret_ref[...] * _silu(rgate_ref[...])
    y_ret = _dot(c.astype(BF16), wbr_ref[...])
    merged = _sigmoid(ga_ref[...]) * y_hg + _sigmoid(gb_ref[...]) * y_ret
    mix = _dot(merged.astype(BF16), wo_ref[...])
    x1 = _layer_norm(DEEPNORM_ALPHA * x_ref[...] + mix, lng_ref[...], lnb_ref[...])
    x1_ref[...] = x1
    x_hi = x1.astype(BF16)
    x_lo = (x1 - x_hi.astype(F32)).astype(BF16)
    wr = wr_ref[...]
    by_hi = _dot(x_hi, wr)
    logit_ref[...] = by_hi[:, :LANES] + by_hi[:, LANES:] + _dot(x_lo, wr[:, :LANES])


def _mixer_out(o_hg, o_ret, proj, x, g_hg, g_ret, wbh, wbr, wo, ln_g, ln_b, w_route):
    t = x.shape[0]
    tm = min(t, MIX_TILE)
    d = D_MODEL
    rows = lambda cb: pl.BlockSpec((tm, d), lambda i: (i, cb))
    full = lambda shape: pl.BlockSpec(shape, lambda i: (0, 0))
    return pl.pallas_call(
        _mix_kernel,
        grid=(t // tm,),
        in_specs=[rows(0), rows(OFF_HGATE // d), rows(0), rows(OFF_RGATE // d), rows(OFF_GA // d),
                  rows(OFF_GB // d), rows(0), full((1, d)), full((1, d)),
                  full((d, d)), full((d, d)), full((d, d)), full((1, d)), full((1, d)), full((d, 2 * LANES))],
        out_specs=[pl.BlockSpec((tm, d), lambda i: (i, 0)), pl.BlockSpec((tm, LANES), lambda i: (i, 0))],
        out_shape=[jax.ShapeDtypeStruct((t, d), F32), jax.ShapeDtypeStruct((t, LANES), F32)],
        compiler_params=pltpu.CompilerParams(dimension_semantics=("parallel",), vmem_limit_bytes=VMEM_LIMIT),
        name="mixer_out",
    )(o_hg, proj, o_ret, proj, proj, proj, x, g_hg, g_ret, wbh, wbr, wo, ln_g, ln_b, w_route)


GROUP_LANE0 = N_EXPERTS


def _route_kernel(logit_ref, bias_ref, tri_ref, idx_ref, wt_ref, cnt_ref, carry_ref):
    i = pl.program_id(0)

    @pl.when(i == 0)
    def _():
        carry_ref[...] = jnp.zeros_like(carry_ref)

    lg = logit_ref[...] + bias_ref[...]
    tm = lg.shape[0]
    lane = lax.broadcasted_iota(jnp.int32, (tm, LANES), 1)
    first_of = lambda hit: jnp.min(jnp.where(hit, lane, LANES), -1, keepdims=True)

    g_mask = (lane >= GROUP_LANE0) & (lane < GROUP_LANE0 + N_GROUPS)
    g_l = jnp.where(g_mask, lg, NEG_BIG)
    g_max = jnp.max(g_l, -1, keepdims=True)
    grp = first_of(g_l == g_max) - GROUP_LANE0
    p_grp = 1.0 / jnp.sum(jnp.where(g_mask, jnp.exp(g_l - g_max), 0.0), -1, keepdims=True)

    e_lo = grp * EXPERTS_PER_GROUP
    e_l = jnp.where((lane >= e_lo) & (lane < e_lo + EXPERTS_PER_GROUP), lg, NEG_BIG)
    m1 = jnp.max(e_l, -1, keepdims=True)
    i1 = first_of(e_l == m1)
    e_l2 = jnp.where(lane == i1, NEG_BIG, e_l)
    m2 = jnp.max(e_l2, -1, keepdims=True)
    i2 = first_of(e_l2 == m2)
    r = jnp.exp(m2 - m1)
    w1 = p_grp / (1.0 + r)
    w2 = p_grp * r / (1.0 + r)

    hit1 = (lane == i1).astype(F32)
    hit2 = (lane == i2).astype(F32)
    both = hit1 + hit2
    before = _dot(tri_ref[...], both.astype(BF16)) + carry_ref[...]
    r1 = jnp.sum(hit1 * before, -1, keepdims=True).astype(jnp.int32)
    r2 = jnp.sum(hit2 * before, -1, keepdims=True).astype(jnp.int32)
    carry_ref[...] += jnp.sum(both, 0, keepdims=True)
    cnt_ref[...] = carry_ref[...]

    slot = lax.broadcasted_iota(jnp.int32, (tm, 8), 1)
    idx_ref[...] = jnp.where(slot == 0, i1, jnp.where(slot == 1, i2, jnp.where(slot == 2, r1, r2)))
    wt_ref[...] = jnp.where(slot == 0, w1, jnp.where(slot == 1, w2, 0.0))


def _route(logits, bias):
    t = logits.shape[0]
    tm = min(t, ROUTE_TILE)
    tri = jnp.asarray(np.tril(np.ones((tm, tm), np.float32), -1)).astype(BF16)
    return pl.pallas_call(
        _route_kernel,
        grid=(t // tm,),
        in_specs=[pl.BlockSpec((tm, LANES), lambda i: (i, 0)), pl.BlockSpec((1, LANES), lambda i: (0, 0)),
                  pl.BlockSpec((tm, tm), lambda i: (0, 0))],
        out_specs=[pl.BlockSpec((tm, 8), lambda i: (i, 0)), pl.BlockSpec((tm, 8), lambda i: (i, 0)),
                   pl.BlockSpec((1, LANES), lambda i: (0, 0))],
        out_shape=[jax.ShapeDtypeStruct((t, 8), jnp.int32), jax.ShapeDtypeStruct((t, 8), F32),
                   jax.ShapeDtypeStruct((1, LANES), F32)],
        scratch_shapes=[pltpu.VMEM((1, LANES), F32)],
        compiler_params=pltpu.CompilerParams(dimension_semantics=("arbitrary",)),
        name="route",
    )(logits, bias, tri)


def _row_copy(src, row, dst, dst_row, sem):
    return pltpu.make_async_copy(src.at[pl.ds(row, 1)], dst.at[pl.ds(dst_row, 1)], sem)


def _expert_kernel(be_ref, rows_ref, nused_ref, x_hbm, w1_ref, w3_ref, w2_ref, y_ref, buf0, buf1, sem):
    i = pl.program_id(0)
    n = pl.num_programs(0)
    n_used = nused_ref[0]
    bufs = (buf0, buf1)
    blk = buf0.shape[0]

    def start_rolled(block, slot):
        def body(r, carry):
            _row_copy(x_hbm, rows_ref[block * blk + r], bufs[slot], r, sem.at[slot]).start()
            return carry
        lax.fori_loop(0, blk, body, 0, unroll=DMA_UNROLL)

    def start_flat(block, slot):
        for r in range(blk):
            _row_copy(x_hbm, rows_ref[block * blk + r], bufs[slot], r, sem.at[slot]).start()

    def drain(slot):
        pltpu.make_async_copy(x_hbm.at[pl.ds(0, blk)], bufs[slot], sem.at[slot]).wait()

    @pl.when(i == 0)
    def _():
        start_rolled(0, 0)

    nxt = jnp.minimum(i + 1, n - 1)
    for p in (0, 1):
        mine = i % 2 == p

        @pl.when(jnp.logical_and(mine, i < n_used))
        def _():
            drain(p)
            start_flat(nxt, 1 - p)
            xs = bufs[p][...].astype(BF16)
            h = _silu(_dot(xs, w1_ref[...])) * _dot(xs, w3_ref[...])
            y_ref[...] = _dot(h.astype(BF16), w2_ref[...])

        @pl.when(jnp.logical_and(mine, i >= n_used))
        def _():
            drain(p)
            start_rolled(nxt, 1 - p)
            y_ref[...] = jnp.zeros_like(y_ref)

        @pl.when(jnp.logical_and(mine, i == n - 1))
        def _():
            drain(1 - p)


def _experts(x1, block_expert, rows, n_used, w1, w3, w2, blk):
    n_blocks = block_expert.shape[0]
    d, de = D_MODEL, D_EXPERT
    grid_spec = pltpu.PrefetchScalarGridSpec(
        num_scalar_prefetch=3,
        grid=(n_blocks,),
        in_specs=[pl.BlockSpec(memory_space=pl.ANY),
                  pl.BlockSpec((None, d, de), lambda i, be, rw, nu: (be[i], 0, 0)),
                  pl.BlockSpec((None, d, de), lambda i, be, rw, nu: (be[i], 0, 0)),
                  pl.BlockSpec((None, de, d), lambda i, be, rw, nu: (be[i], 0, 0))],
        out_specs=pl.BlockSpec((blk, d), lambda i, be, rw, nu: (i, 0)),
        scratch_shapes=[pltpu.VMEM((blk, d), F32), pltpu.VMEM((blk, d), F32), pltpu.SemaphoreType.DMA((2,))],
    )
    return pl.pallas_call(
        _expert_kernel,
        grid_spec=grid_spec,
        out_shape=jax.ShapeDtypeStruct((n_blocks * blk, d), F32),
        compiler_params=pltpu.CompilerParams(dimension_semantics=("arbitrary",), vmem_limit_bytes=VMEM_LIMIT),
        name="experts",
    )(block_expert, rows, n_used, x1, w1, w3, w2)


def _combine_kernel(dest_ref, y_hbm, x1_ref, wt_ref, lng_ref, lnb_ref, o_ref, buf, sem):
    i = pl.program_id(0)
    n = pl.num_programs(0)
    tm = x1_ref.shape[0]

    def start(tile, slot):
        def body(r, carry):
            for k in range(TOP_K):
                _row_copy(y_hbm, dest_ref[(tile * tm + r) * TOP_K + k], buf.at[slot, k], r, sem.at[slot]).start()
            return carry
        lax.fori_loop(0, tm, body, 0, unroll=DMA_UNROLL)

    def drain(slot):
        for k in range(TOP_K):
            pltpu.make_async_copy(y_hbm.at[pl.ds(0, tm)], buf.at[slot, k], sem.at[slot]).wait()

    @pl.when(i == 0)
    def _():
        start(0, 0)

    @pl.when(i + 1 < n)
    def _():
        start(i + 1, (i + 1) % 2)

    slot = i % 2
    drain(slot)
    wt = wt_ref[...]
    moe = wt[:, 0:1] * buf[slot, 0] + wt[:, 1:2] * buf[slot, 1]
    o_ref[...] = _layer_norm(DEEPNORM_ALPHA * x1_ref[...] + moe, lng_ref[...], lnb_ref[...])


def _combine(dest, y, x1, wts, ln_g, ln_b):
    t, d = x1.shape
    tm = min(t, ROUTE_TILE)
    grid_spec = pltpu.PrefetchScalarGridSpec(
        num_scalar_prefetch=1,
        grid=(t // tm,),
        in_specs=[pl.BlockSpec(memory_space=pl.ANY),
                  pl.BlockSpec((tm, d), lambda i, ds: (i, 0)),
                  pl.BlockSpec((tm, 8), lambda i, ds: (i, 0)),
                  pl.BlockSpec((1, d), lambda i, ds: (0, 0)),
                  pl.BlockSpec((1, d), lambda i, ds: (0, 0))],
        out_specs=pl.BlockSpec((tm, d), lambda i, ds: (i, 0)),
        scratch_shapes=[pltpu.VMEM((2, TOP_K, tm, d), F32), pltpu.SemaphoreType.DMA((2,))],
    )
    return pl.pallas_call(
        _combine_kernel,
        grid_spec=grid_spec,
        out_shape=jax.ShapeDtypeStruct((t, d), F32),
        compiler_params=pltpu.CompilerParams(dimension_semantics=("arbitrary",), vmem_limit_bytes=VMEM_LIMIT),
        name="combine",
    )(dest, y, x1, wts, ln_g, ln_b)


def _deinterleave_cols(w):
    dk = RET_DK
    wh = w.reshape(w.shape[0], RET_HEADS, dk // 2, 2)
    return jnp.concatenate([wh[..., 0], wh[..., 1]], axis=-1).reshape(w.shape[0], RET_HEADS * dk)


def kernel(x, positions, w_in, hg_lb_logits, hg_norm_g, ret_norm_g, w_branch_hg, w_branch_ret, w_out,
           ln1_g, ln1_b, w_group, b_group, w_router, b_router, w1, w3, w2, ln2_g, ln2_b):
    b, l, d = x.shape
    t = b * l
    lb_cum = jnp.cumsum(jax.nn.softmax(hg_lb_logits.astype(F32), axis=0), axis=0)
    cos, sin = _rope_tables(positions)
    xf = x.reshape(t, d)
    for layer in range(DEPTH):
        lb = (lb_cum[layer + 1] - lb_cum[0]).reshape(1, -1)
        w = w_in[layer]
        w = jnp.concatenate([w[:, :OFF_RQ], _deinterleave_cols(w[:, OFF_RQ:OFF_RK]),
                             _deinterleave_cols(w[:, OFF_RK:OFF_RV]), w[:, OFF_RV:]], axis=1)
        proj = _matmul(xf.astype(BF16), w.astype(BF16), F32, 1024, 1024, "in_proj")
        proj3 = proj.reshape(b, l, PROJ_DIM)
        o_hg = _hgrn2(proj3, lb).reshape(t, d)
        o_ret = _retention(proj3, cos, sin).reshape(t, d)

        w_route = jnp.zeros((d, LANES), F32)
        w_route = w_route.at[:, :N_EXPERTS].set(w_router[layer]).at[:, GROUP_LANE0:GROUP_LANE0 + N_GROUPS].set(w_group[layer])
        b_route = jnp.zeros((1, LANES), F32)
        b_route = b_route.at[0, :N_EXPERTS].set(b_router[layer]).at[0, GROUP_LANE0:GROUP_LANE0 + N_GROUPS].set(b_group[layer])
        w_route_hi = w_route.astype(BF16)
        w_route = jnp.concatenate([w_route_hi, (w_route - w_route_hi.astype(F32)).astype(BF16)], axis=1)
        row = lambda v: v.reshape(1, -1)
        x1, logits = _mixer_out(o_hg, o_ret, proj, xf, row(hg_norm_g[layer]), row(ret_norm_g[layer]),
                                w_branch_hg[layer].astype(BF16), w_branch_ret[layer].astype(BF16),
                                w_out[layer].astype(BF16), row(ln1_g[layer]), row(ln1_b[layer]), w_route)

        idx, wts, counts = _route(logits, b_route)
        blk = MOE_BLOCK
        counts = counts[0, :N_EXPERTS].astype(jnp.int32)
        padded = (counts + blk - 1) // blk * blk
        pend = jnp.cumsum(padded)
        pstart = pend - padded
        dest = pstart[idx[:, :TOP_K]] + idx[:, TOP_K:2 * TOP_K]
        n_blocks = t * TOP_K // blk + N_EXPERTS
        block_row0 = jnp.arange(n_blocks, dtype=jnp.int32) * blk
        block_expert = jnp.minimum(jnp.sum((pend[None, :] <= block_row0[:, None]).astype(jnp.int32), axis=1),
                                   N_EXPERTS - 1)
        n_used = (pend[-1:] // blk).astype(jnp.int32)
        dest = dest.reshape(-1).astype(jnp.int32)
        tok = jnp.arange(t * TOP_K, dtype=jnp.int32) // TOP_K
        rows = jnp.zeros((n_blocks * blk,), jnp.int32).at[dest].set(tok)
        y = _experts(x1, block_expert, rows, n_used, w1[layer].astype(BF16), w3[layer].astype(BF16),
                     w2[layer].astype(BF16), blk)
        xf = _combine(dest, y, x1, wts, row(ln2_g[layer]), row(ln2_b[layer]))
    return xf.reshape(b, l, d)
```

```python
import functools

import numpy as np
import jax
import jax.numpy as jnp
from jax import lax
from jax.experimental import pallas as pl
from jax.experimental.pallas import tpu as pltpu

D_MODEL = 1024
HG_HEADS = 8
HG_DK = 128
HG_DV = D_MODEL // HG_HEADS
RET_HEADS = 4
RET_DK = 256
RET_DV = 256
ROPE_BASE = 10000.0
OFF_HQ = 0
OFF_HF_FWD = 1024
OFF_HF_BWD = 2048
OFF_HI = 3072
OFF_HGATE = 4096
OFF_RQ = 5120
OFF_RK = 6144
OFF_RV = 7168
OFF_RGATE = 8192
OFF_GA = 9216
OFF_GB = 10240
PROJ_DIM = 11264
N_GROUPS = 4
EXPERTS_PER_GROUP = 8
N_EXPERTS = N_GROUPS * EXPERTS_PER_GROUP
TOP_K = 2
D_EXPERT = 512
DEPTH = 1
DEEPNORM_ALPHA = (2 * DEPTH) ** 0.25
LN_EPS = 1e-5
RMS_EPS = 1e-6

LANES = 128
CHUNK = 64
RET_CHUNK = 128
RET_SPAN = 4
SUB = 16
TIME_BLOCK = 1024
MOE_BLOCK = 256
ROUTE_TILE = 256
DMA_UNROLL = 16
MIX_TILE = 256
VMEM_LIMIT = 56 * 1024 * 1024
NEG_BIG = -1e30
FAST_MIN_GATE = float(np.exp(-60.0 / SUB))
GROW_CAP = 64.0
FAST_SPAN = 4

F32 = jnp.float32
BF16 = jnp.bfloat16
_NT = (((1,), (1,)), ((), ()))
_TN = (((0,), (0,)), ((), ()))


def _sigmoid(v):
    return 1.0 / (1.0 + jnp.exp(-v))


def _silu(v):
    return v * _sigmoid(v)


def _dot(a, b, dims=None):
    if dims is None:
        return jnp.dot(a, b, preferred_element_type=F32)
    return lax.dot_general(a, b, dims, preferred_element_type=F32)


def _layer_norm(z, g, b):
    mu = jnp.mean(z, -1, keepdims=True)
    zc = z - mu
    var = jnp.mean(zc * zc, -1, keepdims=True)
    return zc * lax.rsqrt(var + LN_EPS) * g + b


def _rope_kernel(pos_ref, inv_ref, cos_ref, sin_ref):
    ang = pos_ref[...] * inv_ref[...]
    cos_ref[...] = jnp.cos(ang)
    sin_ref[...] = jnp.sin(ang)


def _rope_tables(positions):
    b, l = positions.shape
    tt = min(l, 1024)
    half = RET_DK // 2
    inv = (1.0 / (ROPE_BASE ** jnp.linspace(0.0, 1.0, half, dtype=F32))).reshape(1, half)
    pos = positions.astype(F32).reshape(b, l, 1)
    spec = pl.BlockSpec((None, tt, half), lambda i, j: (i, j, 0))
    return pl.pallas_call(
        _rope_kernel,
        grid=(b, l // tt),
        in_specs=[pl.BlockSpec((None, tt, 1), lambda i, j: (i, j, 0)),
                  pl.BlockSpec((1, half), lambda i, j: (0, 0))],
        out_specs=[spec, spec],
        out_shape=[jax.ShapeDtypeStruct((b, l, half), F32)] * 2,
        compiler_params=pltpu.CompilerParams(dimension_semantics=("parallel", "parallel")),
        name="rope_tables",
    )(pos, inv)


def _matmul_kernel(x_ref, w_ref, o_ref):
    o_ref[...] = _dot(x_ref[...], w_ref[...]).astype(o_ref.dtype)


def _matmul(x, w, out_dtype, tm, tn, name):
    m, k = x.shape
    n = w.shape[1]
    tm, tn = min(tm, m), min(tn, n)
    return pl.pallas_call(
        _matmul_kernel,
        grid=(n // tn, m // tm),
        in_specs=[pl.BlockSpec((tm, k), lambda j, i: (i, 0)),
                  pl.BlockSpec((k, tn), lambda j, i: (0, j))],
        out_specs=pl.BlockSpec((tm, tn), lambda j, i: (i, j)),
        out_shape=jax.ShapeDtypeStruct((m, n), out_dtype),
        compiler_params=pltpu.CompilerParams(dimension_semantics=("parallel", "parallel"),
                                             vmem_limit_bytes=VMEM_LIMIT),
        name=name,
    )(x, w)


def _hgrn_chunk(hq, hf, v, lb, s_t, tri, rev):
    c = hq.shape[0]
    q = _silu(hq)
    f = lb + (1.0 - lb) * _sigmoid(hf)
    k = 1.0 - f
    lf = jnp.log(f)
    g = jnp.dot(tri, lf, precision=lax.Precision.HIGHEST, preferred_element_type=F32)
    g_excl = g - lf
    g_tot = g[0:1] if rev else g[c - 1:c]

    vb = v.astype(BF16)
    o_inter = _dot((q * jnp.exp(g)).astype(BF16), s_t.astype(BF16), _NT)
    k_dec = (k * jnp.exp(g_tot - g)).astype(BF16)
    s_new = jnp.exp(g_tot) * s_t + _dot(vb, k_dec, _TN)

    ones = jnp.ones((LANES, LANES), BF16)
    t_idx = lax.broadcasted_iota(jnp.int32, (SUB, LANES), 0)
    outs = []
    for i in range(c // SUB):
        r0 = i * SUB
        q_i, k_i, g_i, v_i = q[r0:r0 + SUB], k[r0:r0 + SUB], g[r0:r0 + SUB], v[r0:r0 + SUB]
        lo, hi = (r0 + SUB, c) if rev else (0, r0)
        acc = jnp.zeros((SUB, v.shape[1]), F32)
        if hi > lo:
            first = r0 + SUB - 1 if rev else r0
            g_ref = g_excl[first:first + 1]
            q_t = (q_i * jnp.exp(g_i - g_ref)).astype(BF16)
            k_t = (k[lo:hi] * jnp.exp(g_ref - g[lo:hi])).astype(BF16)
            a = _dot(q_t, k_t, _NT)
            acc = _dot(a.astype(BF16), vb[lo:hi])
        prods = []
        for j in range(SUB):
            valid = (t_idx <= j) if rev else (t_idx >= j)
            e = jnp.exp(jnp.where(valid, g_i - g_i[j:j + 1], NEG_BIG))
            prods.append(q_i * k_i[j:j + 1] * e)
        a_rep = _dot(jnp.concatenate(prods, axis=0).astype(BF16), ones)
        for j in range(SUB):
            acc = acc + a_rep[j * SUB:(j + 1) * SUB] * v_i[j:j + 1]
        outs.append(acc)
    return o_inter + jnp.concatenate(outs, axis=0), s_new


def _hgrn_fast_span(lb, tri, allowed, q_ref, f_ref, v_ref, s_ref, chunk_ids, rev):
    c = CHUNK
    tri_b16 = tri.astype(BF16)
    prepped = []
    for cid in chunk_ids:
        rows = pl.ds(pl.multiple_of(cid * c, c), c)
        q = _silu(q_ref[rows, :])
        f = lb + (1.0 - lb) * _sigmoid(f_ref[rows, :])
        lf = jnp.log(f)
        hi = lf.astype(BF16)
        lo = (lf - hi.astype(F32)).astype(BF16)
        g2 = _dot(tri_b16, jnp.concatenate([hi, lo], axis=1))
        g = g2[:, :LANES] + g2[:, LANES:]
        prepped.append((q, 1.0 - f, lf, g, v_ref[rows, :].astype(BF16)))

    staged = []
    for q, k, lf, g, vb in prepped:
        eg = jnp.exp(g)
        total = eg[0:1] if rev else eg[c - 1:c]
        g_tot = g[0:1] if rev else g[c - 1:c]
        q_inter = (q * eg).astype(BF16)
        d_s = _dot(vb, (k * jnp.exp(g_tot - g)).astype(BF16), _TN)
        g_excl = g - lf
        scores = []
        for i in range(c // SUB):
            r0 = i * SUB
            first = r0 + SUB - 1 if rev else r0
            g_ref = g_excl[first:first + 1]
            q_t = (q[r0:r0 + SUB] * jnp.exp(g[r0:r0 + SUB] - g_ref)).astype(BF16)
            k_t = (k * jnp.exp(jnp.minimum(g_ref - g, GROW_CAP))).astype(BF16)
            scores.append(_dot(q_t, k_t, _NT))
        staged.append((q_inter, d_s, total, jnp.concatenate(scores, axis=0), vb))

    intra = [_dot((sc * allowed).astype(BF16), vb) for _, _, _, sc, vb in staged]

    outs = []
    s_t = s_ref[...]
    for (q_inter, d_s, total, _, _), o_intra in zip(staged, intra):
        outs.append(_dot(q_inter, s_t.astype(BF16), _NT) + o_intra)
        s_t = total * s_t + d_s
    s_ref[...] = s_t
    return outs


def _hgrn_kernel(lb_ref, tri_f_ref, tri_b_ref, qf_ref, ff_ref, vf_ref, qb_ref, fb_ref, vb_ref,
                 o_ref, sf_ref, sb_ref, *, tt):
    n = pl.program_id(2)
    per_block = tt // CHUNK
    n_chunks = pl.num_programs(2) * per_block

    @pl.when(n == 0)
    def _():
        sf_ref[...] = jnp.zeros_like(sf_ref)
        sb_ref[...] = jnp.zeros_like(sb_ref)
        o_ref[...] = jnp.zeros_like(o_ref)

    lb = lb_ref[...]

    def smallest_gate(f_ref):
        return jnp.min(lb + (1.0 - lb) * _sigmoid(jnp.min(f_ref[...], axis=0, keepdims=True)))

    fast_ok = jnp.minimum(smallest_gate(ff_ref), smallest_gate(fb_ref)) > FAST_MIN_GATE

    def out_rows(chunk):
        return pl.ds(pl.multiple_of(chunk * CHUNK, CHUNK), CHUNK)

    @pl.when(fast_ok)
    def _():
        span = min(FAST_SPAN, per_block)

        def body(r, carry):
            ids = [r * span + j for j in range(span)]
            o_f = _hgrn_fast_span(lb, tri_f_ref[...], tri_f_ref[...], qf_ref, ff_ref, vf_ref, sf_ref, ids, False)
            o_b = _hgrn_fast_span(lb, tri_b_ref[...], tri_b_ref[...], qb_ref, fb_ref, vb_ref, sb_ref,
                                  [per_block - 1 - i for i in ids], True)
            for j in range(span):
                it = n * per_block + ids[j]
                o_ref[out_rows(it), :] += o_f[j]
                o_ref[out_rows(n_chunks - 1 - it), :] += o_b[j]
            return carry

        lax.fori_loop(0, per_block // span, body, 0)

    @pl.when(jnp.logical_not(fast_ok))
    def _():
        def body(c, carry):
            it = n * per_block + c
            o_f, s_f = _hgrn_chunk(qf_ref[out_rows(c), :], ff_ref[out_rows(c), :], vf_ref[out_rows(c), :], lb,
                                   sf_ref[...], tri_f_ref[...], False)
            sf_ref[...] = s_f
            o_ref[out_rows(it), :] += o_f
            cb = per_block - 1 - c
            o_b, s_b = _hgrn_chunk(qb_ref[out_rows(cb), :], fb_ref[out_rows(cb), :], vb_ref[out_rows(cb), :], lb,
                                   sb_ref[...], tri_b_ref[...], True)
            sb_ref[...] = s_b
            o_ref[out_rows(n_chunks - 1 - it), :] += o_b
            return carry

        lax.fori_loop(0, per_block, body, 0)


def _hgrn2(proj, lb):
    b, l, _ = proj.shape
    tt = min(l, TIME_BLOCK)
    nb = l // tt
    ones = np.ones((CHUNK, CHUNK), np.float32)
    tri_f, tri_b = jnp.asarray(np.tril(ones)), jnp.asarray(np.triu(ones))

    def col(off, bwd):
        base = off // LANES
        if bwd:
            return pl.BlockSpec((None, tt, LANES), lambda i, h, n: (i, nb - 1 - n, base + h))
        return pl.BlockSpec((None, tt, LANES), lambda i, h, n: (i, n, base + h))

    sq_spec = pl.BlockSpec((CHUNK, CHUNK), lambda i, h, n: (0, 0))
    return pl.pallas_call(
        functools.partial(_hgrn_kernel, tt=tt),
        grid=(b, HG_HEADS, nb),
        in_specs=[pl.BlockSpec((1, LANES), lambda i, h, n: (0, h)), sq_spec, sq_spec,
                  col(OFF_HQ, False), col(OFF_HF_FWD, False), col(OFF_HI, False),
                  col(OFF_HQ, True), col(OFF_HF_BWD, True), col(OFF_HI, True)],
        out_specs=pl.BlockSpec((None, l, HG_DV), lambda i, h, n: (i, 0, h)),
        out_shape=jax.ShapeDtypeStruct((b, l, HG_HEADS * HG_DV), F32),
        scratch_shapes=[pltpu.VMEM((HG_DV, HG_DK), F32), pltpu.VMEM((HG_DV, HG_DK), F32)],
        compiler_params=pltpu.CompilerParams(dimension_semantics=("parallel", "parallel", "arbitrary"),
                                             vmem_limit_bytes=VMEM_LIMIT),
        name="hgrn2",
    )(lb, tri_f, tri_b, proj, proj, proj, proj, proj, proj)


def _rotate(t, cos, sin):
    half = t.shape[1] // 2
    t1, t2 = t[:, :half], t[:, half:]
    return jnp.concatenate([t1 * cos - t2 * sin, t1 * sin + t2 * cos], axis=1)


def _ret_span(q_ref, k_ref, v_ref, cos_ref, sin_ref, r_ref, dmat, cross, kdec, cdec, chunk_ids):
    c = RET_CHUNK
    staged = []
    for cid in chunk_ids:
        rows = pl.ds(pl.multiple_of(cid * c, c), c)
        cos, sin = cos_ref[rows, :], sin_ref[rows, :]
        qr = _rotate(q_ref[rows, :], cos, sin).astype(BF16)
        kr = _rotate(k_ref[rows, :], cos, sin) * (RET_DK ** -0.5)
        vb = v_ref[rows, :].astype(BF16)
        inner = _dot(qr, kr.astype(BF16), _NT)
        d_r = _dot((kr * kdec).astype(BF16), vb, _TN)
        staged.append((qr, vb, inner, d_r))
    intra = [_dot((inner * dmat).astype(BF16), vb) for _, vb, inner, _ in staged]
    outs = []
    r = r_ref[...]
    for (qr, _, _, d_r), o_intra in zip(staged, intra):
        outs.append(o_intra + cross * _dot(qr, r.astype(BF16)))
        r = cdec * r + d_r
    r_ref[...] = r
    return outs


def _ret_kernel(dm_f_ref, dm_b_ref, cr_f_ref, cr_b_ref, kd_f_ref, kd_b_ref, cd_ref,
                qf_ref, kf_ref, vf_ref, cf_ref, sf_ref, qb_ref, kb_ref, vb_ref, cb_ref, sb_ref,
                o_ref, rf_ref, rb_ref, *, tt):
    n = pl.program_id(2)
    per_block = tt // RET_CHUNK
    n_chunks = pl.num_programs(2) * per_block
    span = min(RET_SPAN, per_block)

    @pl.when(n == 0)
    def _():
        rf_ref[...] = jnp.zeros_like(rf_ref)
        rb_ref[...] = jnp.zeros_like(rb_ref)
        o_ref[...] = jnp.zeros_like(o_ref)

    cdec = cd_ref[...]

    def out_rows(chunk):
        return pl.ds(pl.multiple_of(chunk * RET_CHUNK, RET_CHUNK), RET_CHUNK)

    def body(r, carry):
        ids = [r * span + j for j in range(span)]
        o_f = _ret_span(qf_ref, kf_ref, vf_ref, cf_ref, sf_ref, rf_ref, dm_f_ref[...], cr_f_ref[...],
                        kd_f_ref[...], cdec, ids)
        o_b = _ret_span(qb_ref, kb_ref, vb_ref, cb_ref, sb_ref, rb_ref, dm_b_ref[...], cr_b_ref[...],
                        kd_b_ref[...], cdec, [per_block - 1 - i for i in ids])
        for j in range(span):
            it = n * per_block + ids[j]
            o_ref[out_rows(it), :] += o_f[j]
            o_ref[out_rows(n_chunks - 1 - it), :] += o_b[j]
        return carry

    lax.fori_loop(0, per_block // span, body, 0)


def _retention_consts():
    idx = np.arange(RET_CHUNK, dtype=np.float64)
    lg = np.log(1.0 - 2.0 ** (-5.0 - np.arange(RET_HEADS, dtype=np.float64)))[:, None, None]
    rel = idx[:, None] - idx[None, :]
    dm_f = np.where(rel >= 0, np.exp(np.maximum(rel, 0.0) * lg), 0.0)
    dm_b = np.transpose(dm_f, (0, 2, 1))
    wide = np.ones((1, 1, RET_DK))
    cr_f = np.exp((idx + 1.0)[None, :, None] * lg) * wide
    kd_f = np.exp((RET_CHUNK - 1.0 - idx)[None, :, None] * lg) * wide
    cd = np.exp(RET_CHUNK * lg) * wide
    as32 = lambda a: jnp.asarray(a.astype(np.float32))
    return (as32(dm_f), as32(dm_b), as32(cr_f), as32(cr_f[:, ::-1]), as32(kd_f), as32(kd_f[:, ::-1]), as32(cd))


def _retention(proj, cos, sin):
    b, l, _ = proj.shape
    tt = min(l, TIME_BLOCK)
    nb = l // tt
    consts = _retention_consts()

    def tpos(n, bwd):
        return nb - 1 - n if bwd else n

    def col(off, bwd):
        base = off // RET_DK
        return pl.BlockSpec((None, tt, RET_DK), lambda i, h, n: (i, tpos(n, bwd), base + h))

    def tab(bwd):
        return pl.BlockSpec((None, tt, RET_DK // 2), lambda i, h, n: (i, tpos(n, bwd), 0))

    def per_head(shape):
        return pl.BlockSpec((None,) + shape, lambda i, h, n: (h, 0, 0))

    sq, wide, row = (RET_CHUNK, RET_CHUNK), (RET_CHUNK, RET_DK), (1, RET_DK)
    return pl.pallas_call(
        functools.partial(_ret_kernel, tt=tt),
        grid=(b, RET_HEADS, nb),
        in_specs=[per_head(sq), per_head(sq), per_head(wide), per_head(wide), per_head(wide), per_head(wide),
                  per_head(row),
                  col(OFF_RQ, False), col(OFF_RK, False), col(OFF_RV, False), tab(False), tab(False),
                  col(OFF_RQ, True), col(OFF_RK, True), col(OFF_RV, True), tab(True), tab(True)],
        out_specs=pl.BlockSpec((None, l, RET_DV), lambda i, h, n: (i, 0, h)),
        out_shape=jax.ShapeDtypeStruct((b, l, RET_HEADS * RET_DV), F32),
        scratch_shapes=[pltpu.VMEM((RET_DK, RET_DV), F32), pltpu.VMEM((RET_DK, RET_DV), F32)],
        compiler_params=pltpu.CompilerParams(dimension_semantics=("parallel", "parallel", "arbitrary"),
                                             vmem_limit_bytes=VMEM_LIMIT),
        name="retention",
    )(*consts, proj, proj, proj, cos, sin, proj, proj, proj, cos, sin)


def _norm_heads(o, n_heads):
    d = o.shape[1] // n_heads
    parts = []
    for h in range(n_heads):
        oh = o[:, h * d:(h + 1) * d]
        parts.append(oh * lax.rsqrt(jnp.mean(oh * oh, -1, keepdims=True) + RMS_EPS))
    return jnp.concatenate(parts, axis=1)


def _mix_kernel(ohg_ref, hgate_ref, oret_ref, rgate_ref, ga_ref, gb_ref, x_ref, ghg_ref, gret_ref,
                wbh_ref, wbr_ref, wo_ref, lng_ref, lnb_ref, wr_ref, x1_ref, logit_ref):
    a = _norm_heads(ohg_ref[...], HG_HEADS) * ghg_ref[...] * _silu(hgate_ref[...])
    y_hg = _dot(a.astype(BF16), wbh_ref[...])
    c = _norm_heads(oret_ref[...], RET_HEADS) * gret_ref[...] * _silu(rgate_ref[...])
    y_ret = _dot(c.astype(BF16), wbr_ref[...])
    merged = _sigmoid(ga_ref[...]) * y_hg + _sigmoid(gb_ref[...]) * y_ret
    mix = _dot(merged.astype(BF16), wo_ref[...])
    x1 = _layer_norm(DEEPNORM_ALPHA * x_ref[...] + mix, lng_ref[...], lnb_ref[...])
    x1_ref[...] = x1
    x_hi = x1.astype(BF16)
    x_lo = (x1 - x_hi.astype(F32)).astype(BF16)
    w = wr_ref[...]
    w_hi = w.astype(BF16)
    w_lo = (w - w_hi.astype(F32)).astype(BF16)
    by_hi = _dot(x_hi, jnp.concatenate([w_hi, w_lo], axis=1))
    logit_ref[...] = by_hi[:, :LANES] + by_hi[:, LANES:] + _dot(x_lo, w_hi)


def _mixer_out(o_hg, o_ret, proj, x, g_hg, g_ret, wbh, wbr, wo, ln_g, ln_b, w_route):
    t = x.shape[0]
    tm = min(t, MIX_TILE)
    d = D_MODEL
    rows = lambda cb: pl.BlockSpec((tm, d), lambda i: (i, cb))
    full = lambda shape: pl.BlockSpec(shape, lambda i: (0, 0))
    return pl.pallas_call(
        _mix_kernel,
        grid=(t // tm,),
        in_specs=[rows(0), rows(OFF_HGATE // d), rows(0), rows(OFF_RGATE // d), rows(OFF_GA // d),
                  rows(OFF_GB // d), rows(0), full((1, d)), full((1, d)),
                  full((d, d)), full((d, d)), full((d, d)), full((1, d)), full((1, d)), full((d, LANES))],
        out_specs=[pl.BlockSpec((tm, d), lambda i: (i, 0)), pl.BlockSpec((tm, LANES), lambda i: (i, 0))],
        out_shape=[jax.ShapeDtypeStruct((t, d), F32), jax.ShapeDtypeStruct((t, LANES), F32)],
        compiler_params=pltpu.CompilerParams(dimension_semantics=("parallel",), vmem_limit_bytes=VMEM_LIMIT),
        name="mixer_out",
    )(o_hg, proj, o_ret, proj, proj, proj, x, g_hg, g_ret, wbh, wbr, wo, ln_g, ln_b, w_route)


GROUP_LANE0 = N_EXPERTS


def _route_kernel(logit_ref, bias_ref, tri_ref, idx_ref, wt_ref, cnt_ref, carry_ref):
    i = pl.program_id(0)

    @pl.when(i == 0)
    def _():
        carry_ref[...] = jnp.zeros_like(carry_ref)

    lg = logit_ref[...] + bias_ref[...]
    tm = lg.shape[0]
    lane = lax.broadcasted_iota(jnp.int32, (tm, LANES), 1)
    first_of = lambda hit: jnp.min(jnp.where(hit, lane, LANES), -1, keepdims=True)

    g_mask = (lane >= GROUP_LANE0) & (lane < GROUP_LANE0 + N_GROUPS)
    g_l = jnp.where(g_mask, lg, NEG_BIG)
    g_max = jnp.max(g_l, -1, keepdims=True)
    grp = first_of(g_l == g_max) - GROUP_LANE0
    p_grp = 1.0 / jnp.sum(jnp.where(g_mask, jnp.exp(g_l - g_max), 0.0), -1, keepdims=True)

    e_lo = grp * EXPERTS_PER_GROUP
    e_l = jnp.where((lane >= e_lo) & (lane < e_lo + EXPERTS_PER_GROUP), lg, NEG_BIG)
    m1 = jnp.max(e_l, -1, keepdims=True)
    i1 = first_of(e_l == m1)
    e_l2 = jnp.where(lane == i1, NEG_BIG, e_l)
    m2 = jnp.max(e_l2, -1, keepdims=True)
    i2 = first_of(e_l2 == m2)
    r = jnp.exp(m2 - m1)
    w1 = p_grp / (1.0 + r)
    w2 = p_grp * r / (1.0 + r)

    hit1 = (lane == i1).astype(F32)
    hit2 = (lane == i2).astype(F32)
    both = hit1 + hit2
    before = _dot(tri_ref[...], both.astype(BF16)) + carry_ref[...]
    r1 = jnp.sum(hit1 * before, -1, keepdims=True).astype(jnp.int32)
    r2 = jnp.sum(hit2 * before, -1, keepdims=True).astype(jnp.int32)
    carry_ref[...] += jnp.sum(both, 0, keepdims=True)
    cnt_ref[...] = carry_ref[...]

    slot = lax.broadcasted_iota(jnp.int32, (tm, 8), 1)
    idx_ref[...] = jnp.where(slot == 0, i1, jnp.where(slot == 1, i2, jnp.where(slot == 2, r1, r2)))
    wt_ref[...] = jnp.where(slot == 0, w1, jnp.where(slot == 1, w2, 0.0))


def _route(logits, bias):
    t = logits.shape[0]
    tm = min(t, ROUTE_TILE)
    tri = jnp.asarray(np.tril(np.ones((tm, tm), np.float32), -1)).astype(BF16)
    return pl.pallas_call(
        _route_kernel,
        grid=(t // tm,),
        in_specs=[pl.BlockSpec((tm, LANES), lambda i: (i, 0)), pl.BlockSpec((1, LANES), lambda i: (0, 0)),
                  pl.BlockSpec((tm, tm), lambda i: (0, 0))],
        out_specs=[pl.BlockSpec((tm, 8), lambda i: (i, 0)), pl.BlockSpec((tm, 8), lambda i: (i, 0)),
                   pl.BlockSpec((1, LANES), lambda i: (0, 0))],
        out_shape=[jax.ShapeDtypeStruct((t, 8), jnp.int32), jax.ShapeDtypeStruct((t, 8), F32),
                   jax.ShapeDtypeStruct((1, LANES), F32)],
        scratch_shapes=[pltpu.VMEM((1, LANES), F32)],
        compiler_params=pltpu.CompilerParams(dimension_semantics=("arbitrary",)),
        name="route",
    )(logits, bias, tri)


def _row_copy(src, row, dst, dst_row, sem):
    return pltpu.make_async_copy(src.at[pl.ds(row, 1)], dst.at[pl.ds(dst_row, 1)], sem)


def _expert_kernel(be_ref, rows_ref, nused_ref, x_hbm, w1_ref, w3_ref, w2_ref, y_ref, buf, sem):
    i = pl.program_id(0)
    n_used = nused_ref[0]
    blk = buf.shape[1]

    def start(block, slot):
        def body(r, carry):
            _row_copy(x_hbm, rows_ref[block * blk + r], buf.at[slot], r, sem.at[slot]).start()
            return carry
        lax.fori_loop(0, blk, body, 0, unroll=DMA_UNROLL)

    @pl.when(i == 0)
    def _():
        start(0, 0)

    @pl.when(i + 1 < n_used)
    def _():
        start(i + 1, (i + 1) % 2)

    @pl.when(i < n_used)
    def _():
        slot = i % 2
        pltpu.make_async_copy(x_hbm.at[pl.ds(0, blk)], buf.at[slot], sem.at[slot]).wait()
        xs = buf[slot].astype(BF16)
        h = _silu(_dot(xs, w1_ref[...])) * _dot(xs, w3_ref[...])
        y_ref[...] = _dot(h.astype(BF16), w2_ref[...])

    @pl.when(i >= n_used)
    def _():
        y_ref[...] = jnp.zeros_like(y_ref)


def _experts(x1, block_expert, rows, n_used, w1, w3, w2, blk):
    n_blocks = block_expert.shape[0]
    d, de = D_MODEL, D_EXPERT
    grid_spec = pltpu.PrefetchScalarGridSpec(
        num_scalar_prefetch=3,
        grid=(n_blocks,),
        in_specs=[pl.BlockSpec(memory_space=pl.ANY),
                  pl.BlockSpec((None, d, de), lambda i, be, rw, nu: (be[i], 0, 0)),
                  pl.BlockSpec((None, d, de), lambda i, be, rw, nu: (be[i], 0, 0)),
                  pl.BlockSpec((None, de, d), lambda i, be, rw, nu: (be[i], 0, 0))],
        out_specs=pl.BlockSpec((blk, d), lambda i, be, rw, nu: (i, 0)),
        scratch_shapes=[pltpu.VMEM((2, blk, d), F32), pltpu.SemaphoreType.DMA((2,))],
    )
    return pl.pallas_call(
        _expert_kernel,
        grid_spec=grid_spec,
        out_shape=jax.ShapeDtypeStruct((n_blocks * blk, d), F32),
        compiler_params=pltpu.CompilerParams(dimension_semantics=("arbitrary",), vmem_limit_bytes=VMEM_LIMIT),
        name="experts",
    )(block_expert, rows, n_used, x1, w1, w3, w2)


def _combine_kernel(dest_ref, y_hbm, x1_ref, wt_ref, lng_ref, lnb_ref, o_ref, buf, sem):
    i = pl.program_id(0)
    n = pl.num_programs(0)
    tm = x1_ref.shape[0]

    def start(tile, slot):
        def body(r, carry):
            for k in range(TOP_K):
                _row_copy(y_hbm, dest_ref[(tile * tm + r) * TOP_K + k], buf.at[slot, k], r, sem.at[slot]).start()
            return carry
        lax.fori_loop(0, tm, body, 0, unroll=DMA_UNROLL)

    def drain(slot):
        for k in range(TOP_K):
            pltpu.make_async_copy(y_hbm.at[pl.ds(0, tm)], buf.at[slot, k], sem.at[slot]).wait()

    @pl.when(i == 0)
    def _():
        start(0, 0)

    @pl.when(i + 1 < n)
    def _():
        start(i + 1, (i + 1) % 2)

    slot = i % 2
    drain(slot)
    wt = wt_ref[...]
    moe = wt[:, 0:1] * buf[slot, 0] + wt[:, 1:2] * buf[slot, 1]
    o_ref[...] = _layer_norm(DEEPNORM_ALPHA * x1_ref[...] + moe, lng_ref[...], lnb_ref[...])


def _combine(dest, y, x1, wts, ln_g, ln_b):
    t, d = x1.shape
    tm = min(t, ROUTE_TILE)
    grid_spec = pltpu.PrefetchScalarGridSpec(
        num_scalar_prefetch=1,
        grid=(t // tm,),
        in_specs=[pl.BlockSpec(memory_space=pl.ANY),
                  pl.BlockSpec((tm, d), lambda i, ds: (i, 0)),
                  pl.BlockSpec((tm, 8), lambda i, ds: (i, 0)),
                  pl.BlockSpec((1, d), lambda i, ds: (0, 0)),
                  pl.BlockSpec((1, d), lambda i, ds: (0, 0))],
        out_specs=pl.BlockSpec((tm, d), lambda i, ds: (i, 0)),
        scratch_shapes=[pltpu.VMEM((2, TOP_K, tm, d), F32), pltpu.SemaphoreType.DMA((2,))],
    )
    return pl.pallas_call(
        _combine_kernel,
        grid_spec=grid_spec,
        out_shape=jax.ShapeDtypeStruct((t, d), F32),
        compiler_params=pltpu.CompilerParams(dimension_semantics=("arbitrary",), vmem_limit_bytes=VMEM_LIMIT),
        name="combine",
    )(dest, y, x1, wts, ln_g, ln_b)


def _deinterleave_cols(w):
    dk = RET_DK
    wh = w.reshape(w.shape[0], RET_HEADS, dk // 2, 2)
    return jnp.concatenate([wh[..., 0], wh[..., 1]], axis=-1).reshape(w.shape[0], RET_HEADS * dk)


def kernel(x, positions, w_in, hg_lb_logits, hg_norm_g, ret_norm_g, w_branch_hg, w_branch_ret, w_out,
           ln1_g, ln1_b, w_group, b_group, w_router, b_router, w1, w3, w2, ln2_g, ln2_b):
    b, l, d = x.shape
    t = b * l
    lb_cum = jnp.cumsum(jax.nn.softmax(hg_lb_logits.astype(F32), axis=0), axis=0)
    cos, sin = _rope_tables(positions)
    xf = x.reshape(t, d)
    for layer in range(DEPTH):
        lb = (lb_cum[layer + 1] - lb_cum[0]).reshape(1, -1)
        w = w_in[layer]
        w = jnp.concatenate([w[:, :OFF_RQ], _deinterleave_cols(w[:, OFF_RQ:OFF_RK]),
                             _deinterleave_cols(w[:, OFF_RK:OFF_RV]), w[:, OFF_RV:]], axis=1)
        proj = _matmul(xf.astype(BF16), w.astype(BF16), F32, 1024, 1024, "in_proj")
        proj3 = proj.reshape(b, l, PROJ_DIM)
        o_hg = _hgrn2(proj3, lb).reshape(t, d)
        o_ret = _retention(proj3, cos, sin).reshape(t, d)

        w_route = jnp.zeros((d, LANES), F32)
        w_route = w_route.at[:, :N_EXPERTS].set(w_router[layer]).at[:, GROUP_LANE0:GROUP_LANE0 + N_GROUPS].set(w_group[layer])
        b_route = jnp.zeros((1, LANES), F32)
        b_route = b_route.at[0, :N_EXPERTS].set(b_router[layer]).at[0, GROUP_LANE0:GROUP_LANE0 + N_GROUPS].set(b_group[layer])
        row = lambda v: v.reshape(1, -1)
        x1, logits = _mixer_out(o_hg, o_ret, proj, xf, row(hg_norm_g[layer]), row(ret_norm_g[layer]),
                                w_branch_hg[layer].astype(BF16), w_branch_ret[layer].astype(BF16),
                                w_out[layer].astype(BF16), row(ln1_g[layer]), row(ln1_b[layer]), w_route)

        idx, wts, counts = _route(logits, b_route)
        blk = MOE_BLOCK
        counts = counts[0, :N_EXPERTS].astype(jnp.int32)
        padded = (counts + blk - 1) // blk * blk
        pend = jnp.cumsum(padded)
        pstart = pend - padded
        dest = pstart[idx[:, :TOP_K]] + idx[:, TOP_K:2 * TOP_K]
        n_blocks = t * TOP_K // blk + N_EXPERTS
        block_row0 = jnp.arange(n_blocks, dtype=jnp.int32) * blk
        block_expert = jnp.minimum(jnp.sum((pend[None, :] <= block_row0[:, None]).astype(jnp.int32), axis=1),
                                   N_EXPERTS - 1)
        n_used = (pend[-1:] // blk).astype(jnp.int32)
        dest = dest.reshape(-1).astype(jnp.int32)
        tok = jnp.arange(t * TOP_K, dtype=jnp.int32) // TOP_K
        rows = jnp.zeros((n_blocks * blk,), jnp.int32).at[dest].set(tok)
        y = _experts(x1, block_expert, rows, n_used, w1[layer].astype(BF16), w3[layer].astype(BF16),
                     w2[layer].astype(BF16), blk)
        xf = _combine(dest, y, x1, wts, row(ln2_g[layer]), row(ln2_b[layer]))
    return xf.reshape(b, l, d)
```

```python
import functools

import numpy as np
import jax
import jax.numpy as jnp
from jax import lax
from jax.experimental import pallas as pl
from jax.experimental.pallas import tpu as pltpu

D_MODEL = 1024
HG_HEADS = 8
HG_DK = 128
HG_DV = D_MODEL // HG_HEADS
RET_HEADS = 4
RET_DK = 256
RET_DV = 256
ROPE_BASE = 10000.0
OFF_HQ = 0
OFF_HF_FWD = 1024
OFF_HF_BWD = 2048
OFF_HI = 3072
OFF_HGATE = 4096
OFF_RQ = 5120
OFF_RK = 6144
OFF_RV = 7168
OFF_RGATE = 8192
OFF_GA = 9216
OFF_GB = 10240
PROJ_DIM = 11264
N_GROUPS = 4
EXPERTS_PER_GROUP = 8
N_EXPERTS = N_GROUPS * EXPERTS_PER_GROUP
TOP_K = 2
D_EXPERT = 512
DEPTH = 1
DEEPNORM_ALPHA = (2 * DEPTH) ** 0.25
LN_EPS = 1e-5
RMS_EPS = 1e-6

LANES = 128
CHUNK = 64
RET_CHUNK = 128
RET_SPAN = 4
SUB = 16
TIME_BLOCK = 1024
MOE_BLOCK = 256
ROUTE_TILE = 256
DMA_UNROLL = 16
MIX_TILE = 256
VMEM_LIMIT = 56 * 1024 * 1024
NEG_BIG = -1e30
FAST_MIN_GATE = float(np.exp(-60.0 / SUB))
GROW_CAP = 64.0
FAST_SPAN = 8

F32 = jnp.float32
BF16 = jnp.bfloat16
_NT = (((1,), (1,)), ((), ()))
_TN = (((0,), (0,)), ((), ()))


def _sigmoid(v):
    return 1.0 / (1.0 + jnp.exp(-v))


def _silu(v):
    return v * _sigmoid(v)


def _dot(a, b, dims=None):
    if dims is None:
        return jnp.dot(a, b, preferred_element_type=F32)
    return lax.dot_general(a, b, dims, preferred_element_type=F32)


def _layer_norm(z, g, b):
    mu = jnp.mean(z, -1, keepdims=True)
    zc = z - mu
    var = jnp.mean(zc * zc, -1, keepdims=True)
    return zc * lax.rsqrt(var + LN_EPS) * g + b


def _rope_kernel(pos_ref, inv_ref, cos_ref, sin_ref):
    ang = pos_ref[...] * inv_ref[...]
    cos_ref[...] = jnp.cos(ang)
    sin_ref[...] = jnp.sin(ang)


def _rope_tables(positions):
    b, l = positions.shape
    tt = min(l, 1024)
    half = RET_DK // 2
    inv = (1.0 / (ROPE_BASE ** jnp.linspace(0.0, 1.0, half, dtype=F32))).reshape(1, half)
    pos = positions.astype(F32).reshape(b, l, 1)
    spec = pl.BlockSpec((None, tt, half), lambda i, j: (i, j, 0))
    return pl.pallas_call(
        _rope_kernel,
        grid=(b, l // tt),
        in_specs=[pl.BlockSpec((None, tt, 1), lambda i, j: (i, j, 0)),
                  pl.BlockSpec((1, half), lambda i, j: (0, 0))],
        out_specs=[spec, spec],
        out_shape=[jax.ShapeDtypeStruct((b, l, half), F32)] * 2,
        compiler_params=pltpu.CompilerParams(dimension_semantics=("parallel", "parallel")),
        name="rope_tables",
    )(pos, inv)


def _matmul_kernel(x_ref, w_ref, o_ref):
    o_ref[...] = _dot(x_ref[...], w_ref[...]).astype(o_ref.dtype)


def _matmul(x, w, out_dtype, tm, tn, name):
    m, k = x.shape
    n = w.shape[1]
    tm, tn = min(tm, m), min(tn, n)
    return pl.pallas_call(
        _matmul_kernel,
        grid=(n // tn, m // tm),
        in_specs=[pl.BlockSpec((tm, k), lambda j, i: (i, 0)),
                  pl.BlockSpec((k, tn), lambda j, i: (0, j))],
        out_specs=pl.BlockSpec((tm, tn), lambda j, i: (i, j)),
        out_shape=jax.ShapeDtypeStruct((m, n), out_dtype),
        compiler_params=pltpu.CompilerParams(dimension_semantics=("parallel", "parallel"),
                                             vmem_limit_bytes=VMEM_LIMIT),
        name=name,
    )(x, w)


def _hgrn_chunk(hq, hf, v, lb, s_t, tri, rev):
    c = hq.shape[0]
    q = _silu(hq)
    f = lb + (1.0 - lb) * _sigmoid(hf)
    k = 1.0 - f
    lf = jnp.log(f)
    g = jnp.dot(tri, lf, precision=lax.Precision.HIGHEST, preferred_element_type=F32)
    g_excl = g - lf
    g_tot = g[0:1] if rev else g[c - 1:c]

    vb = v.astype(BF16)
    o_inter = _dot((q * jnp.exp(g)).astype(BF16), s_t.astype(BF16), _NT)
    k_dec = (k * jnp.exp(g_tot - g)).astype(BF16)
    s_new = jnp.exp(g_tot) * s_t + _dot(vb, k_dec, _TN)

    ones = jnp.ones((LANES, LANES), BF16)
    t_idx = lax.broadcasted_iota(jnp.int32, (SUB, LANES), 0)
    outs = []
    for i in range(c // SUB):
        r0 = i * SUB
        q_i, k_i, g_i, v_i = q[r0:r0 + SUB], k[r0:r0 + SUB], g[r0:r0 + SUB], v[r0:r0 + SUB]
        lo, hi = (r0 + SUB, c) if rev else (0, r0)
        acc = jnp.zeros((SUB, v.shape[1]), F32)
        if hi > lo:
            first = r0 + SUB - 1 if rev else r0
            g_ref = g_excl[first:first + 1]
            q_t = (q_i * jnp.exp(g_i - g_ref)).astype(BF16)
            k_t = (k[lo:hi] * jnp.exp(g_ref - g[lo:hi])).astype(BF16)
            a = _dot(q_t, k_t, _NT)
            acc = _dot(a.astype(BF16), vb[lo:hi])
        prods = []
        for j in range(SUB):
            valid = (t_idx <= j) if rev else (t_idx >= j)
            e = jnp.exp(jnp.where(valid, g_i - g_i[j:j + 1], NEG_BIG))
            prods.append(q_i * k_i[j:j + 1] * e)
        a_rep = _dot(jnp.concatenate(prods, axis=0).astype(BF16), ones)
        for j in range(SUB):
            acc = acc + a_rep[j * SUB:(j + 1) * SUB] * v_i[j:j + 1]
        outs.append(acc)
    return o_inter + jnp.concatenate(outs, axis=0), s_new


def _hgrn_fast_span(lb, tri, allowed, q_ref, f_ref, v_ref, s_ref, chunk_ids, rev):
    c = CHUNK
    tri_b16 = tri.astype(BF16)
    prepped = []
    for cid in chunk_ids:
        rows = pl.ds(pl.multiple_of(cid * c, c), c)
        q = _silu(q_ref[rows, :])
        f = lb + (1.0 - lb) * _sigmoid(f_ref[rows, :])
        lf = jnp.log(f)
        hi = lf.astype(BF16)
        lo = (lf - hi.astype(F32)).astype(BF16)
        g2 = _dot(tri_b16, jnp.concatenate([hi, lo], axis=1))
        g = g2[:, :LANES] + g2[:, LANES:]
        prepped.append((q, 1.0 - f, lf, g, v_ref[rows, :].astype(BF16)))

    staged = []
    for q, k, lf, g, vb in prepped:
        eg = jnp.exp(g)
        total = eg[0:1] if rev else eg[c - 1:c]
        g_tot = g[0:1] if rev else g[c - 1:c]
        q_inter = (q * eg).astype(BF16)
        d_s = _dot(vb, (k * jnp.exp(g_tot - g)).astype(BF16), _TN)
        g_excl = g - lf
        scores = []
        for i in range(c // SUB):
            r0 = i * SUB
            first = r0 + SUB - 1 if rev else r0
            g_ref = g_excl[first:first + 1]
            q_t = (q[r0:r0 + SUB] * jnp.exp(g[r0:r0 + SUB] - g_ref)).astype(BF16)
            k_t = (k * jnp.exp(jnp.minimum(g_ref - g, GROW_CAP))).astype(BF16)
            scores.append(_dot(q_t, k_t, _NT))
        staged.append((q_inter, d_s, total, jnp.concatenate(scores, axis=0), vb))

    intra = [_dot((sc * allowed).astype(BF16), vb) for _, _, _, sc, vb in staged]

    outs = []
    s_t = s_ref[...]
    for (q_inter, d_s, total, _, _), o_intra in zip(staged, intra):
        outs.append(_dot(q_inter, s_t.astype(BF16), _NT) + o_intra)
        s_t = total * s_t + d_s
    s_ref[...] = s_t
    return outs


def _hgrn_kernel(lb_ref, tri_f_ref, tri_b_ref, qf_ref, ff_ref, vf_ref, qb_ref, fb_ref, vb_ref,
                 o_ref, sf_ref, sb_ref, *, tt):
    n = pl.program_id(2)
    per_block = tt // CHUNK
    n_chunks = pl.num_programs(2) * per_block

    @pl.when(n == 0)
    def _():
        sf_ref[...] = jnp.zeros_like(sf_ref)
        sb_ref[...] = jnp.zeros_like(sb_ref)
        o_ref[...] = jnp.zeros_like(o_ref)

    lb = lb_ref[...]

    def smallest_gate(f_ref):
        return jnp.min(lb + (1.0 - lb) * _sigmoid(jnp.min(f_ref[...], axis=0, keepdims=True)))

    fast_ok = jnp.minimum(smallest_gate(ff_ref), smallest_gate(fb_ref)) > FAST_MIN_GATE

    def out_rows(chunk):
        return pl.ds(pl.multiple_of(chunk * CHUNK, CHUNK), CHUNK)

    @pl.when(fast_ok)
    def _():
        span = min(FAST_SPAN, per_block)

        def body(r, carry):
            ids = [r * span + j for j in range(span)]
            o_f = _hgrn_fast_span(lb, tri_f_ref[...], tri_f_ref[...], qf_ref, ff_ref, vf_ref, sf_ref, ids, False)
            o_b = _hgrn_fast_span(lb, tri_b_ref[...], tri_b_ref[...], qb_ref, fb_ref, vb_ref, sb_ref,
                                  [per_block - 1 - i for i in ids], True)
            for j in range(span):
                it = n * per_block + ids[j]
                o_ref[out_rows(it), :] += o_f[j]
                o_ref[out_rows(n_chunks - 1 - it), :] += o_b[j]
            return carry

        lax.fori_loop(0, per_block // span, body, 0)

    @pl.when(jnp.logical_not(fast_ok))
    def _():
        def body(c, carry):
            it = n * per_block + c
            o_f, s_f = _hgrn_chunk(qf_ref[out_rows(c), :], ff_ref[out_rows(c), :], vf_ref[out_rows(c), :], lb,
                                   sf_ref[...], tri_f_ref[...], False)
            sf_ref[...] = s_f
            o_ref[out_rows(it), :] += o_f
            cb = per_block - 1 - c
            o_b, s_b = _hgrn_chunk(qb_ref[out_rows(cb), :], fb_ref[out_rows(cb), :], vb_ref[out_rows(cb), :], lb,
                                   sb_ref[...], tri_b_ref[...], True)
            sb_ref[...] = s_b
            o_ref[out_rows(n_chunks - 1 - it), :] += o_b
            return carry

        lax.fori_loop(0, per_block, body, 0)


def _hgrn2(proj, lb):
    b, l, _ = proj.shape
    tt = min(l, TIME_BLOCK)
    nb = l // tt
    ones = np.ones((CHUNK, CHUNK), np.float32)
    tri_f, tri_b = jnp.asarray(np.tril(ones)), jnp.asarray(np.triu(ones))

    def col(off, bwd):
        base = off // LANES
        if bwd:
            return pl.BlockSpec((None, tt, LANES), lambda i, h, n: (i, nb - 1 - n, base + h))
        return pl.BlockSpec((None, tt, LANES), lambda i, h, n: (i, n, base + h))

    sq_spec = pl.BlockSpec((CHUNK, CHUNK), lambda i, h, n: (0, 0))
    return pl.pallas_call(
        functools.partial(_hgrn_kernel, tt=tt),
        grid=(b, HG_HEADS, nb),
        in_specs=[pl.BlockSpec((1, LANES), lambda i, h, n: (0, h)), sq_spec, sq_spec,
                  col(OFF_HQ, False), col(OFF_HF_FWD, False), col(OFF_HI, False),
                  col(OFF_HQ, True), col(OFF_HF_BWD, True), col(OFF_HI, True)],
        out_specs=pl.BlockSpec((None, l, HG_DV), lambda i, h, n: (i, 0, h)),
        out_shape=jax.ShapeDtypeStruct((b, l, HG_HEADS * HG_DV), F32),
        scratch_shapes=[pltpu.VMEM((HG_DV, HG_DK), F32), pltpu.VMEM((HG_DV, HG_DK), F32)],
        compiler_params=pltpu.CompilerParams(dimension_semantics=("parallel", "parallel", "arbitrary"),
                                             vmem_limit_bytes=VMEM_LIMIT),
        name="hgrn2",
    )(lb, tri_f, tri_b, proj, proj, proj, proj, proj, proj)


def _rotate(t, cos, sin):
    half = t.shape[1] // 2
    t1, t2 = t[:, :half], t[:, half:]
    return jnp.concatenate([t1 * cos - t2 * sin, t1 * sin + t2 * cos], axis=1)


def _ret_span(q_ref, k_ref, v_ref, cos_ref, sin_ref, r_ref, dmat, cross, kdec, cdec, chunk_ids):
    c = RET_CHUNK
    staged = []
    for cid in chunk_ids:
        rows = pl.ds(pl.multiple_of(cid * c, c), c)
        cos, sin = cos_ref[rows, :], sin_ref[rows, :]
        qr = _rotate(q_ref[rows, :], cos, sin).astype(BF16)
        kr = _rotate(k_ref[rows, :], cos, sin) * (RET_DK ** -0.5)
        vb = v_ref[rows, :].astype(BF16)
        inner = _dot(qr, kr.astype(BF16), _NT)
        d_r = _dot((kr * kdec).astype(BF16), vb, _TN)
        staged.append((qr, vb, inner, d_r))
    intra = [_dot((inner * dmat).astype(BF16), vb) for _, vb, inner, _ in staged]
    outs = []
    r = r_ref[...]
    for (qr, _, _, d_r), o_intra in zip(staged, intra):
        outs.append(o_intra + cross * _dot(qr, r.astype(BF16)))
        r = cdec * r + d_r
    r_ref[...] = r
    return outs


def _ret_kernel(dm_f_ref, dm_b_ref, cr_f_ref, cr_b_ref, kd_f_ref, kd_b_ref, cd_ref,
                qf_ref, kf_ref, vf_ref, cf_ref, sf_ref, qb_ref, kb_ref, vb_ref, cb_ref, sb_ref,
                o_ref, rf_ref, rb_ref, *, tt):
    n = pl.program_id(2)
    per_block = tt // RET_CHUNK
    n_chunks = pl.num_programs(2) * per_block
    span = min(RET_SPAN, per_block)

    @pl.when(n == 0)
    def _():
        rf_ref[...] = jnp.zeros_like(rf_ref)
        rb_ref[...] = jnp.zeros_like(rb_ref)
        o_ref[...] = jnp.zeros_like(o_ref)

    cdec = cd_ref[...]

    def out_rows(chunk):
        return pl.ds(pl.multiple_of(chunk * RET_CHUNK, RET_CHUNK), RET_CHUNK)

    def body(r, carry):
        ids = [r * span + j for j in range(span)]
        o_f = _ret_span(qf_ref, kf_ref, vf_ref, cf_ref, sf_ref, rf_ref, dm_f_ref[...], cr_f_ref[...],
                        kd_f_ref[...], cdec, ids)
        o_b = _ret_span(qb_ref, kb_ref, vb_ref, cb_ref, sb_ref, rb_ref, dm_b_ref[...], cr_b_ref[...],
                        kd_b_ref[...], cdec, [per_block - 1 - i for i in ids])
        for j in range(span):
            it = n * per_block + ids[j]
            o_ref[out_rows(it), :] += o_f[j]
            o_ref[out_rows(n_chunks - 1 - it), :] += o_b[j]
        return carry

    lax.fori_loop(0, per_block // span, body, 0)


def _retention_consts():
    idx = np.arange(RET_CHUNK, dtype=np.float64)
    lg = np.log(1.0 - 2.0 ** (-5.0 - np.arange(RET_HEADS, dtype=np.float64)))[:, None, None]
    rel = idx[:, None] - idx[None, :]
    dm_f = np.where(rel >= 0, np.exp(np.maximum(rel, 0.0) * lg), 0.0)
    dm_b = np.transpose(dm_f, (0, 2, 1))
    wide = np.ones((1, 1, RET_DK))
    cr_f = np.exp((idx + 1.0)[None, :, None] * lg) * wide
    kd_f = np.exp((RET_CHUNK - 1.0 - idx)[None, :, None] * lg) * wide
    cd = np.exp(RET_CHUNK * lg) * wide
    as32 = lambda a: jnp.asarray(a.astype(np.float32))
    return (as32(dm_f), as32(dm_b), as32(cr_f), as32(cr_f[:, ::-1]), as32(kd_f), as32(kd_f[:, ::-1]), as32(cd))


def _retention(proj, cos, sin):
    b, l, _ = proj.shape
    tt = min(l, TIME_BLOCK)
    nb = l // tt
    consts = _retention_consts()

    def tpos(n, bwd):
        return nb - 1 - n if bwd else n

    def col(off, bwd):
        base = off // RET_DK
        return pl.BlockSpec((None, tt, RET_DK), lambda i, h, n: (i, tpos(n, bwd), base + h))

    def tab(bwd):
        return pl.BlockSpec((None, tt, RET_DK // 2), lambda i, h, n: (i, tpos(n, bwd), 0))

    def per_head(shape):
        return pl.BlockSpec((None,) + shape, lambda i, h, n: (h, 0, 0))

    sq, wide, row = (RET_CHUNK, RET_CHUNK), (RET_CHUNK, RET_DK), (1, RET_DK)
    return pl.pallas_call(
        functools.partial(_ret_kernel, tt=tt),
        grid=(b, RET_HEADS, nb),
        in_specs=[per_head(sq), per_head(sq), per_head(wide), per_head(wide), per_head(wide), per_head(wide),
                  per_head(row),
                  col(OFF_RQ, False), col(OFF_RK, False), col(OFF_RV, False), tab(False), tab(False),
                  col(OFF_RQ, True), col(OFF_RK, True), col(OFF_RV, True), tab(True), tab(True)],
        out_specs=pl.BlockSpec((None, l, RET_DV), lambda i, h, n: (i, 0, h)),
        out_shape=jax.ShapeDtypeStruct((b, l, RET_HEADS * RET_DV), F32),
        scratch_shapes=[pltpu.VMEM((RET_DK, RET_DV), F32), pltpu.VMEM((RET_DK, RET_DV), F32)],
        compiler_params=pltpu.CompilerParams(dimension_semantics=("parallel", "parallel", "arbitrary"),
                                             vmem_limit_bytes=VMEM_LIMIT),
        name="retention",
    )(*consts, proj, proj, proj, cos, sin, proj, proj, proj, cos, sin)


def _norm_heads(o, n_heads):
    d = o.shape[1] // n_heads
    parts = []
    for h in range(n_heads):
        oh = o[:, h * d:(h + 1) * d]
        parts.append(oh * lax.rsqrt(jnp.mean(oh * oh, -1, keepdims=True) + RMS_EPS))
    return jnp.concatenate(parts, axis=1)


def _mix_kernel(ohg_ref, hgate_ref, oret_ref, rgate_ref, ga_ref, gb_ref, x_ref, ghg_ref, gret_ref,
                wbh_ref, wbr_ref, wo_ref, lng_ref, lnb_ref, wr_ref, x1_ref, logit_ref):
    a = _norm_heads(ohg_ref[...], HG_HEADS) * ghg_ref[...] * _silu(hgate_ref[...])
    y_hg = _dot(a.astype(BF16), wbh_ref[...])
    c = _norm_heads(oret_ref[...], RET_HEADS) * gret_ref[...] * _silu(rgate_ref[...])
    y_ret = _dot(c.astype(BF16), wbr_ref[...])
    merged = _sigmoid(ga_ref[...]) * y_hg + _sigmoid(gb_ref[...]) * y_ret
    mix = _dot(merged.astype(BF16), wo_ref[...])
    x1 = _layer_norm(DEEPNORM_ALPHA * x_ref[...] + mix, lng_ref[...], lnb_ref[...])
    x1_ref[...] = x1
    x_hi = x1.astype(BF16)
    x_lo = (x1 - x_hi.astype(F32)).astype(BF16)
    w = wr_ref[...]
    w_hi = w.astype(BF16)
    w_lo = (w - w_hi.astype(F32)).astype(BF16)
    by_hi = _dot(x_hi, jnp.concatenate([w_hi, w_lo], axis=1))
    logit_ref[...] = by_hi[:, :LANES] + by_hi[:, LANES:] + _dot(x_lo, w_hi)


def _mixer_out(o_hg, o_ret, proj, x, g_hg, g_ret, wbh, wbr, wo, ln_g, ln_b, w_route):
    t = x.shape[0]
    tm = min(t, MIX_TILE)
    d = D_MODEL
    rows = lambda cb: pl.BlockSpec((tm, d), lambda i: (i, cb))
    full = lambda shape: pl.BlockSpec(shape, lambda i: (0, 0))
    return pl.pallas_call(
        _mix_kernel,
        grid=(t // tm,),
        in_specs=[rows(0), rows(OFF_HGATE // d), rows(0), rows(OFF_RGATE // d), rows(OFF_GA // d),
                  rows(OFF_GB // d), rows(0), full((1, d)), full((1, d)),
                  full((d, d)), full((d, d)), full((d, d)), full((1, d)), full((1, d)), full((d, LANES))],
        out_specs=[pl.BlockSpec((tm, d), lambda i: (i, 0)), pl.BlockSpec((tm, LANES), lambda i: (i, 0))],
        out_shape=[jax.ShapeDtypeStruct((t, d), F32), jax.ShapeDtypeStruct((t, LANES), F32)],
        compiler_params=pltpu.CompilerParams(dimension_semantics=("parallel",), vmem_limit_bytes=VMEM_LIMIT),
        name="mixer_out",
    )(o_hg, proj, o_ret, proj, proj, proj, x, g_hg, g_ret, wbh, wbr, wo, ln_g, ln_b, w_route)


GROUP_LANE0 = N_EXPERTS


def _route_kernel(logit_ref, bias_ref, tri_ref, idx_ref, wt_ref, cnt_ref, carry_ref):
    i = pl.program_id(0)

    @pl.when(i == 0)
    def _():
        carry_ref[...] = jnp.zeros_like(carry_ref)

    lg = logit_ref[...] + bias_ref[...]
    tm = lg.shape[0]
    lane = lax.broadcasted_iota(jnp.int32, (tm, LANES), 1)
    first_of = lambda hit: jnp.min(jnp.where(hit, lane, LANES), -1, keepdims=True)

    g_mask = (lane >= GROUP_LANE0) & (lane < GROUP_LANE0 + N_GROUPS)
    g_l = jnp.where(g_mask, lg, NEG_BIG)
    g_max = jnp.max(g_l, -1, keepdims=True)
    grp = first_of(g_l == g_max) - GROUP_LANE0
    p_grp = 1.0 / jnp.sum(jnp.where(g_mask, jnp.exp(g_l - g_max), 0.0), -1, keepdims=True)

    e_lo = grp * EXPERTS_PER_GROUP
    e_l = jnp.where((lane >= e_lo) & (lane < e_lo + EXPERTS_PER_GROUP), lg, NEG_BIG)
    m1 = jnp.max(e_l, -1, keepdims=True)
    i1 = first_of(e_l == m1)
    e_l2 = jnp.where(lane == i1, NEG_BIG, e_l)
    m2 = jnp.max(e_l2, -1, keepdims=True)
    i2 = first_of(e_l2 == m2)
    r = jnp.exp(m2 - m1)
    w1 = p_grp / (1.0 + r)
    w2 = p_grp * r / (1.0 + r)

    hit1 = (lane == i1).astype(F32)
    hit2 = (lane == i2).astype(F32)
    both = hit1 + hit2
    before = _dot(tri_ref[...], both.astype(BF16)) + carry_ref[...]
    r1 = jnp.sum(hit1 * before, -1, keepdims=True).astype(jnp.int32)
    r2 = jnp.sum(hit2 * before, -1, keepdims=True).astype(jnp.int32)
    carry_ref[...] += jnp.sum(both, 0, keepdims=True)
    cnt_ref[...] = carry_ref[...]

    slot = lax.broadcasted_iota(jnp.int32, (tm, 8), 1)
    idx_ref[...] = jnp.where(slot == 0, i1, jnp.where(slot == 1, i2, jnp.where(slot == 2, r1, r2)))
    wt_ref[...] = jnp.where(slot == 0, w1, jnp.where(slot == 1, w2, 0.0))


def _route(logits, bias):
    t = logits.shape[0]
    tm = min(t, ROUTE_TILE)
    tri = jnp.asarray(np.tril(np.ones((tm, tm), np.float32), -1)).astype(BF16)
    return pl.pallas_call(
        _route_kernel,
        grid=(t // tm,),
        in_specs=[pl.BlockSpec((tm, LANES), lambda i: (i, 0)), pl.BlockSpec((1, LANES), lambda i: (0, 0)),
                  pl.BlockSpec((tm, tm), lambda i: (0, 0))],
        out_specs=[pl.BlockSpec((tm, 8), lambda i: (i, 0)), pl.BlockSpec((tm, 8), lambda i: (i, 0)),
                   pl.BlockSpec((1, LANES), lambda i: (0, 0))],
        out_shape=[jax.ShapeDtypeStruct((t, 8), jnp.int32), jax.ShapeDtypeStruct((t, 8), F32),
                   jax.ShapeDtypeStruct((1, LANES), F32)],
        scratch_shapes=[pltpu.VMEM((1, LANES), F32)],
        compiler_params=pltpu.CompilerParams(dimension_semantics=("arbitrary",)),
        name="route",
    )(logits, bias, tri)


def _row_copy(src, row, dst, dst_row, sem):
    return pltpu.make_async_copy(src.at[pl.ds(row, 1)], dst.at[pl.ds(dst_row, 1)], sem)


def _expert_kernel(be_ref, rows_ref, nused_ref, x_hbm, w1_ref, w3_ref, w2_ref, y_ref, buf, sem):
    i = pl.program_id(0)
    n_used = nused_ref[0]
    blk = buf.shape[1]

    def start(block, slot):
        def body(r, carry):
            _row_copy(x_hbm, rows_ref[block * blk + r], buf.at[slot], r, sem.at[slot]).start()
            return carry
        lax.fori_loop(0, blk, body, 0, unroll=DMA_UNROLL)

    @pl.when(i == 0)
    def _():
        start(0, 0)

    @pl.when(i + 1 < n_used)
    def _():
        start(i + 1, (i + 1) % 2)

    @pl.when(i < n_used)
    def _():
        slot = i % 2
        pltpu.make_async_copy(x_hbm.at[pl.ds(0, blk)], buf.at[slot], sem.at[slot]).wait()
        xs = buf[slot].astype(BF16)
        h = _silu(_dot(xs, w1_ref[...])) * _dot(xs, w3_ref[...])
        y_ref[...] = _dot(h.astype(BF16), w2_ref[...])

    @pl.when(i >= n_used)
    def _():
        y_ref[...] = jnp.zeros_like(y_ref)


def _experts(x1, block_expert, rows, n_used, w1, w3, w2, blk):
    n_blocks = block_expert.shape[0]
    d, de = D_MODEL, D_EXPERT
    grid_spec = pltpu.PrefetchScalarGridSpec(
        num_scalar_prefetch=3,
        grid=(n_blocks,),
        in_specs=[pl.BlockSpec(memory_space=pl.ANY),
                  pl.BlockSpec((None, d, de), lambda i, be, rw, nu: (be[i], 0, 0)),
                  pl.BlockSpec((None, d, de), lambda i, be, rw, nu: (be[i], 0, 0)),
                  pl.BlockSpec((None, de, d), lambda i, be, rw, nu: (be[i], 0, 0))],
        out_specs=pl.BlockSpec((blk, d), lambda i, be, rw, nu: (i, 0)),
        scratch_shapes=[pltpu.VMEM((2, blk, d), F32), pltpu.SemaphoreType.DMA((2,))],
    )
    return pl.pallas_call(
        _expert_kernel,
        grid_spec=grid_spec,
        out_shape=jax.ShapeDtypeStruct((n_blocks * blk, d), F32),
        compiler_params=pltpu.CompilerParams(dimension_semantics=("arbitrary",), vmem_limit_bytes=VMEM_LIMIT),
        name="experts",
    )(block_expert, rows, n_used, x1, w1, w3, w2)


def _combine_kernel(dest_ref, y_hbm, x1_ref, wt_ref, lng_ref, lnb_ref, o_ref, buf, sem):
    i = pl.program_id(0)
    n = pl.num_programs(0)
    tm = x1_ref.shape[0]

    def start(tile, slot):
        def body(r, carry):
            for k in range(TOP_K):
                _row_copy(y_hbm, dest_ref[(tile * tm + r) * TOP_K + k], buf.at[slot, k], r, sem.at[slot]).start()
            return carry
        lax.fori_loop(0, tm, body, 0, unroll=DMA_UNROLL)

    def drain(slot):
        for k in range(TOP_K):
            pltpu.make_async_copy(y_hbm.at[pl.ds(0, tm)], buf.at[slot, k], sem.at[slot]).wait()

    @pl.when(i == 0)
    def _():
        start(0, 0)

    @pl.when(i + 1 < n)
    def _():
        start(i + 1, (i + 1) % 2)

    slot = i % 2
    drain(slot)
    wt = wt_ref[...]
    moe = wt[:, 0:1] * buf[slot, 0] + wt[:, 1:2] * buf[slot, 1]
    o_ref[...] = _layer_norm(DEEPNORM_ALPHA * x1_ref[...] + moe, lng_ref[...], lnb_ref[...])


def _combine(dest, y, x1, wts, ln_g, ln_b):
    t, d = x1.shape
    tm = min(t, ROUTE_TILE)
    grid_spec = pltpu.PrefetchScalarGridSpec(
        num_scalar_prefetch=1,
        grid=(t // tm,),
        in_specs=[pl.BlockSpec(memory_space=pl.ANY),
                  pl.BlockSpec((tm, d), lambda i, ds: (i, 0)),
                  pl.BlockSpec((tm, 8), lambda i, ds: (i, 0)),
                  pl.BlockSpec((1, d), lambda i, ds: (0, 0)),
                  pl.BlockSpec((1, d), lambda i, ds: (0, 0))],
        out_specs=pl.BlockSpec((tm, d), lambda i, ds: (i, 0)),
        scratch_shapes=[pltpu.VMEM((2, TOP_K, tm, d), F32), pltpu.SemaphoreType.DMA((2,))],
    )
    return pl.pallas_call(
        _combine_kernel,
        grid_spec=grid_spec,
        out_shape=jax.ShapeDtypeStruct((t, d), F32),
        compiler_params=pltpu.CompilerParams(dimension_semantics=("arbitrary",), vmem_limit_bytes=VMEM_LIMIT),
        name="combine",
    )(dest, y, x1, wts, ln_g, ln_b)


def _deinterleave_cols(w):
    dk = RET_DK
    wh = w.reshape(w.shape[0], RET_HEADS, dk // 2, 2)
    return jnp.concatenate([wh[..., 0], wh[..., 1]], axis=-1).reshape(w.shape[0], RET_HEADS * dk)


def kernel(x, positions, w_in, hg_lb_logits, hg_norm_g, ret_norm_g, w_branch_hg, w_branch_ret, w_out,
           ln1_g, ln1_b, w_group, b_group, w_router, b_router, w1, w3, w2, ln2_g, ln2_b):
    b, l, d = x.shape
    t = b * l
    lb_cum = jnp.cumsum(jax.nn.softmax(hg_lb_logits.astype(F32), axis=0), axis=0)
    cos, sin = _rope_tables(positions)
    xf = x.reshape(t, d)
    for layer in range(DEPTH):
        lb = (lb_cum[layer + 1] - lb_cum[0]).reshape(1, -1)
        w = w_in[layer]
        w = jnp.concatenate([w[:, :OFF_RQ], _deinterleave_cols(w[:, OFF_RQ:OFF_RK]),
                             _deinterleave_cols(w[:, OFF_RK:OFF_RV]), w[:, OFF_RV:]], axis=1)
        proj = _matmul(xf.astype(BF16), w.astype(BF16), F32, 1024, 1024, "in_proj")
        proj3 = proj.reshape(b, l, PROJ_DIM)
        o_hg = _hgrn2(proj3, lb).reshape(t, d)
        o_ret = _retention(proj3, cos, sin).reshape(t, d)

        w_route = jnp.zeros((d, LANES), F32)
        w_route = w_route.at[:, :N_EXPERTS].set(w_router[layer]).at[:, GROUP_LANE0:GROUP_LANE0 + N_GROUPS].set(w_group[layer])
        b_route = jnp.zeros((1, LANES), F32)
        b_route = b_route.at[0, :N_EXPERTS].set(b_router[layer]).at[0, GROUP_LANE0:GROUP_LANE0 + N_GROUPS].set(b_group[layer])
        row = lambda v: v.reshape(1, -1)
        x1, logits = _mixer_out(o_hg, o_ret, proj, xf, row(hg_norm_g[layer]), row(ret_norm_g[layer]),
                                w_branch_hg[layer].astype(BF16), w_branch_ret[layer].astype(BF16),
                                w_out[layer].astype(BF16), row(ln1_g[layer]), row(ln1_b[layer]), w_route)

        idx, wts, counts = _route(logits, b_route)
        blk = MOE_BLOCK
        counts = counts[0, :N_EXPERTS].astype(jnp.int32)
        padded = (counts + blk - 1) // blk * blk
        pend = jnp.cumsum(padded)
        pstart = pend - padded
        dest = pstart[idx[:, :TOP_K]] + idx[:, TOP_K:2 * TOP_K]
        n_blocks = t * TOP_K // blk + N_EXPERTS
        block_row0 = jnp.arange(n_blocks, dtype=jnp.int32) * blk
        block_expert = jnp.minimum(jnp.sum((pend[None, :] <= block_row0[:, None]).astype(jnp.int32), axis=1),
                                   N_EXPERTS - 1)
        n_used = (pend[-1:] // blk).astype(jnp.int32)
        dest = dest.reshape(-1).astype(jnp.int32)
        tok = jnp.arange(t * TOP_K, dtype=jnp.int32) // TOP_K
        rows = jnp.zeros((n_blocks * blk,), jnp.int32).at[dest].set(tok)
        y = _experts(x1, block_expert, rows, n_used, w1[layer].astype(BF16), w3[layer].astype(BF16),
                     w2[layer].astype(BF16), blk)
        xf = _combine(dest, y, x1, wts, row(ln2_g[layer]), row(ln2_b[layer]))
    return xf.reshape(b, l, d)
```

```python
import functools

import numpy as np
import jax
import jax.numpy as jnp
from jax import lax
from jax.experimental import pallas as pl
from jax.experimental.pallas import tpu as pltpu

D_MODEL = 1024
HG_HEADS = 8
HG_DK = 128
HG_DV = D_MODEL // HG_HEADS
RET_HEADS = 4
RET_DK = 256
RET_DV = 256
ROPE_BASE = 10000.0
OFF_HQ = 0
OFF_HF_FWD = 1024
OFF_HF_BWD = 2048
OFF_HI = 3072
OFF_HGATE = 4096
OFF_RQ = 5120
OFF_RK = 6144
OFF_RV = 7168
OFF_RGATE = 8192
OFF_GA = 9216
OFF_GB = 10240
PROJ_DIM = 11264
N_GROUPS = 4
EXPERTS_PER_GROUP = 8
N_EXPERTS = N_GROUPS * EXPERTS_PER_GROUP
TOP_K = 2
D_EXPERT = 512
DEPTH = 1
DEEPNORM_ALPHA = (2 * DEPTH) ** 0.25
LN_EPS = 1e-5
RMS_EPS = 1e-6

LANES = 128
CHUNK = 64
RET_CHUNK = 128
RET_SPAN = 4
SUB = 16
TIME_BLOCK = 1024
MOE_BLOCK = 256
ROUTE_TILE = 256
DMA_UNROLL = 16
MIX_TILE = 256
VMEM_LIMIT = 56 * 1024 * 1024
NEG_BIG = -1e30
FAST_MIN_GATE = float(np.exp(-60.0 / SUB))
GROW_CAP = 64.0
FAST_SPAN = 16

F32 = jnp.float32
BF16 = jnp.bfloat16
_NT = (((1,), (1,)), ((), ()))
_TN = (((0,), (0,)), ((), ()))


def _sigmoid(v):
    return 1.0 / (1.0 + jnp.exp(-v))


def _silu(v):
    return v * _sigmoid(v)


def _dot(a, b, dims=None):
    if dims is None:
        return jnp.dot(a, b, preferred_element_type=F32)
    return lax.dot_general(a, b, dims, preferred_element_type=F32)


def _layer_norm(z, g, b):
    mu = jnp.mean(z, -1, keepdims=True)
    zc = z - mu
    var = jnp.mean(zc * zc, -1, keepdims=True)
    return zc * lax.rsqrt(var + LN_EPS) * g + b


def _rope_kernel(pos_ref, inv_ref, cos_ref, sin_ref):
    ang = pos_ref[...] * inv_ref[...]
    cos_ref[...] = jnp.cos(ang)
    sin_ref[...] = jnp.sin(ang)


def _rope_tables(positions):
    b, l = positions.shape
    tt = min(l, 1024)
    half = RET_DK // 2
    inv = (1.0 / (ROPE_BASE ** jnp.linspace(0.0, 1.0, half, dtype=F32))).reshape(1, half)
    pos = positions.astype(F32).reshape(b, l, 1)
    spec = pl.BlockSpec((None, tt, half), lambda i, j: (i, j, 0))
    return pl.pallas_call(
        _rope_kernel,
        grid=(b, l // tt),
        in_specs=[pl.BlockSpec((None, tt, 1), lambda i, j: (i, j, 0)),
                  pl.BlockSpec((1, half), lambda i, j: (0, 0))],
        out_specs=[spec, spec],
        out_shape=[jax.ShapeDtypeStruct((b, l, half), F32)] * 2,
        compiler_params=pltpu.CompilerParams(dimension_semantics=("parallel", "parallel")),
        name="rope_tables",
    )(pos, inv)


def _matmul_kernel(x_ref, w_ref, o_ref):
    o_ref[...] = _dot(x_ref[...], w_ref[...]).astype(o_ref.dtype)


def _matmul(x, w, out_dtype, tm, tn, name):
    m, k = x.shape
    n = w.shape[1]
    tm, tn = min(tm, m), min(tn, n)
    return pl.pallas_call(
        _matmul_kernel,
        grid=(n // tn, m // tm),
        in_specs=[pl.BlockSpec((tm, k), lambda j, i: (i, 0)),
                  pl.BlockSpec((k, tn), lambda j, i: (0, j))],
        out_specs=pl.BlockSpec((tm, tn), lambda j, i: (i, j)),
        out_shape=jax.ShapeDtypeStruct((m, n), out_dtype),
        compiler_params=pltpu.CompilerParams(dimension_semantics=("parallel", "parallel"),
                                             vmem_limit_bytes=VMEM_LIMIT),
        name=name,
    )(x, w)


def _hgrn_chunk(hq, hf, v, lb, s_t, tri, rev):
    c = hq.shape[0]
    q = _silu(hq)
    f = lb + (1.0 - lb) * _sigmoid(hf)
    k = 1.0 - f
    lf = jnp.log(f)
    g = jnp.dot(tri, lf, precision=lax.Precision.HIGHEST, preferred_element_type=F32)
    g_excl = g - lf
    g_tot = g[0:1] if rev else g[c - 1:c]

    vb = v.astype(BF16)
    o_inter = _dot((q * jnp.exp(g)).astype(BF16), s_t.astype(BF16), _NT)
    k_dec = (k * jnp.exp(g_tot - g)).astype(BF16)
    s_new = jnp.exp(g_tot) * s_t + _dot(vb, k_dec, _TN)

    ones = jnp.ones((LANES, LANES), BF16)
    t_idx = lax.broadcasted_iota(jnp.int32, (SUB, LANES), 0)
    outs = []
    for i in range(c // SUB):
        r0 = i * SUB
        q_i, k_i, g_i, v_i = q[r0:r0 + SUB], k[r0:r0 + SUB], g[r0:r0 + SUB], v[r0:r0 + SUB]
        lo, hi = (r0 + SUB, c) if rev else (0, r0)
        acc = jnp.zeros((SUB, v.shape[1]), F32)
        if hi > lo:
            first = r0 + SUB - 1 if rev else r0
            g_ref = g_excl[first:first + 1]
            q_t = (q_i * jnp.exp(g_i - g_ref)).astype(BF16)
            k_t = (k[lo:hi] * jnp.exp(g_ref - g[lo:hi])).astype(BF16)
            a = _dot(q_t, k_t, _NT)
            acc = _dot(a.astype(BF16), vb[lo:hi])
        prods = []
        for j in range(SUB):
            valid = (t_idx <= j) if rev else (t_idx >= j)
            e = jnp.exp(jnp.where(valid, g_i - g_i[j:j + 1], NEG_BIG))
            prods.append(q_i * k_i[j:j + 1] * e)
        a_rep = _dot(jnp.concatenate(prods, axis=0).astype(BF16), ones)
        for j in range(SUB):
            acc = acc + a_rep[j * SUB:(j + 1) * SUB] * v_i[j:j + 1]
        outs.append(acc)
    return o_inter + jnp.concatenate(outs, axis=0), s_new


def _hgrn_fast_span(lb, tri, allowed, q_ref, f_ref, v_ref, s_ref, chunk_ids, rev):
    c = CHUNK
    tri_b16 = tri.astype(BF16)
    prepped = []
    for cid in chunk_ids:
        rows = pl.ds(pl.multiple_of(cid * c, c), c)
        q = _silu(q_ref[rows, :])
        f = lb + (1.0 - lb) * _sigmoid(f_ref[rows, :])
        lf = jnp.log(f)
        hi = lf.astype(BF16)
        lo = (lf - hi.astype(F32)).astype(BF16)
        g2 = _dot(tri_b16, jnp.concatenate([hi, lo], axis=1))
        g = g2[:, :LANES] + g2[:, LANES:]
        prepped.append((q, 1.0 - f, lf, g, v_ref[rows, :].astype(BF16)))

    staged = []
    for q, k, lf, g, vb in prepped:
        eg = jnp.exp(g)
        total = eg[0:1] if rev else eg[c - 1:c]
        g_tot = g[0:1] if rev else g[c - 1:c]
        q_inter = (q * eg).astype(BF16)
        d_s = _dot(vb, (k * jnp.exp(g_tot - g)).astype(BF16), _TN)
        g_excl = g - lf
        scores = []
        for i in range(c // SUB):
            r0 = i * SUB
            first = r0 + SUB - 1 if rev else r0
            g_ref = g_excl[first:first + 1]
            q_t = (q[r0:r0 + SUB] * jnp.exp(g[r0:r0 + SUB] - g_ref)).astype(BF16)
            k_t = (k * jnp.exp(jnp.minimum(g_ref - g, GROW_CAP))).astype(BF16)
            scores.append(_dot(q_t, k_t, _NT))
        staged.append((q_inter, d_s, total, jnp.concatenate(scores, axis=0), vb))

    intra = [_dot((sc * allowed).astype(BF16), vb) for _, _, _, sc, vb in staged]

    outs = []
    s_t = s_ref[...]
    for (q_inter, d_s, total, _, _), o_intra in zip(staged, intra):
        outs.append(_dot(q_inter, s_t.astype(BF16), _NT) + o_intra)
        s_t = total * s_t + d_s
    s_ref[...] = s_t
    return outs


def _hgrn_kernel(lb_ref, tri_f_ref, tri_b_ref, qf_ref, ff_ref, vf_ref, qb_ref, fb_ref, vb_ref,
                 o_ref, sf_ref, sb_ref, *, tt):
    n = pl.program_id(2)
    per_block = tt // CHUNK
    n_chunks = pl.num_programs(2) * per_block

    @pl.when(n == 0)
    def _():
        sf_ref[...] = jnp.zeros_like(sf_ref)
        sb_ref[...] = jnp.zeros_like(sb_ref)
        o_ref[...] = jnp.zeros_like(o_ref)

    lb = lb_ref[...]

    def smallest_gate(f_ref):
        return jnp.min(lb + (1.0 - lb) * _sigmoid(jnp.min(f_ref[...], axis=0, keepdims=True)))

    fast_ok = jnp.minimum(smallest_gate(ff_ref), smallest_gate(fb_ref)) > FAST_MIN_GATE

    def out_rows(chunk):
        return pl.ds(pl.multiple_of(chunk * CHUNK, CHUNK), CHUNK)

    @pl.when(fast_ok)
    def _():
        span = min(FAST_SPAN, per_block)

        def body(r, carry):
            ids = [r * span + j for j in range(span)]
            o_f = _hgrn_fast_span(lb, tri_f_ref[...], tri_f_ref[...], qf_ref, ff_ref, vf_ref, sf_ref, ids, False)
            o_b = _hgrn_fast_span(lb, tri_b_ref[...], tri_b_ref[...], qb_ref, fb_ref, vb_ref, sb_ref,
                                  [per_block - 1 - i for i in ids], True)
            for j in range(span):
                it = n * per_block + ids[j]
                o_ref[out_rows(it), :] += o_f[j]
                o_ref[out_rows(n_chunks - 1 - it), :] += o_b[j]
            return carry

        lax.fori_loop(0, per_block // span, body, 0)

    @pl.when(jnp.logical_not(fast_ok))
    def _():
        def body(c, carry):
            it = n * per_block + c
            o_f, s_f = _hgrn_chunk(qf_ref[out_rows(c), :], ff_ref[out_rows(c), :], vf_ref[out_rows(c), :], lb,
                                   sf_ref[...], tri_f_ref[...], False)
            sf_ref[...] = s_f
            o_ref[out_rows(it), :] += o_f
            cb = per_block - 1 - c
            o_b, s_b = _hgrn_chunk(qb_ref[out_rows(cb), :], fb_ref[out_rows(cb), :], vb_ref[out_rows(cb), :], lb,
                                   sb_ref[...], tri_b_ref[...], True)
            sb_ref[...] = s_b
            o_ref[out_rows(n_chunks - 1 - it), :] += o_b
            return carry

        lax.fori_loop(0, per_block, body, 0)


def _hgrn2(proj, lb):
    b, l, _ = proj.shape
    tt = min(l, TIME_BLOCK)
    nb = l // tt
    ones = np.ones((CHUNK, CHUNK), np.float32)
    tri_f, tri_b = jnp.asarray(np.tril(ones)), jnp.asarray(np.triu(ones))

    def col(off, bwd):
        base = off // LANES
        if bwd:
            return pl.BlockSpec((None, tt, LANES), lambda i, h, n: (i, nb - 1 - n, base + h))
        return pl.BlockSpec((None, tt, LANES), lambda i, h, n: (i, n, base + h))

    sq_spec = pl.BlockSpec((CHUNK, CHUNK), lambda i, h, n: (0, 0))
    return pl.pallas_call(
        functools.partial(_hgrn_kernel, tt=tt),
        grid=(b, HG_HEADS, nb),
        in_specs=[pl.BlockSpec((1, LANES), lambda i, h, n: (0, h)), sq_spec, sq_spec,
                  col(OFF_HQ, False), col(OFF_HF_FWD, False), col(OFF_HI, False),
                  col(OFF_HQ, True), col(OFF_HF_BWD, True), col(OFF_HI, True)],
        out_specs=pl.BlockSpec((None, l, HG_DV), lambda i, h, n: (i, 0, h)),
        out_shape=jax.ShapeDtypeStruct((b, l, HG_HEADS * HG_DV), F32),
        scratch_shapes=[pltpu.VMEM((HG_DV, HG_DK), F32), pltpu.VMEM((HG_DV, HG_DK), F32)],
        compiler_params=pltpu.CompilerParams(dimension_semantics=("parallel", "parallel", "arbitrary"),
                                             vmem_limit_bytes=VMEM_LIMIT),
        name="hgrn2",
    )(lb, tri_f, tri_b, proj, proj, proj, proj, proj, proj)


def _rotate(t, cos, sin):
    half = t.shape[1] // 2
    t1, t2 = t[:, :half], t[:, half:]
    return jnp.concatenate([t1 * cos - t2 * sin, t1 * sin + t2 * cos], axis=1)


def _ret_span(q_ref, k_ref, v_ref, cos_ref, sin_ref, r_ref, dmat, cross, kdec, cdec, chunk_ids):
    c = RET_CHUNK
    staged = []
    for cid in chunk_ids:
        rows = pl.ds(pl.multiple_of(cid * c, c), c)
        cos, sin = cos_ref[rows, :], sin_ref[rows, :]
        qr = _rotate(q_ref[rows, :], cos, sin).astype(BF16)
        kr = _rotate(k_ref[rows, :], cos, sin) * (RET_DK ** -0.5)
        vb = v_ref[rows, :].astype(BF16)
        inner = _dot(qr, kr.astype(BF16), _NT)
        d_r = _dot((kr * kdec).astype(BF16), vb, _TN)
        staged.append((qr, vb, inner, d_r))
    intra = [_dot((inner * dmat).astype(BF16), vb) for _, vb, inner, _ in staged]
    outs = []
    r = r_ref[...]
    for (qr, _, _, d_r), o_intra in zip(staged, intra):
        outs.append(o_intra + cross * _dot(qr, r.astype(BF16)))
        r = cdec * r + d_r
    r_ref[...] = r
    return outs


def _ret_kernel(dm_f_ref, dm_b_ref, cr_f_ref, cr_b_ref, kd_f_ref, kd_b_ref, cd_ref,
                qf_ref, kf_ref, vf_ref, cf_ref, sf_ref, qb_ref, kb_ref, vb_ref, cb_ref, sb_ref,
                o_ref, rf_ref, rb_ref, *, tt):
    n = pl.program_id(2)
    per_block = tt // RET_CHUNK
    n_chunks = pl.num_programs(2) * per_block
    span = min(RET_SPAN, per_block)

    @pl.when(n == 0)
    def _():
        rf_ref[...] = jnp.zeros_like(rf_ref)
        rb_ref[...] = jnp.zeros_like(rb_ref)
        o_ref[...] = jnp.zeros_like(o_ref)

    cdec = cd_ref[...]

    def out_rows(chunk):
        return pl.ds(pl.multiple_of(chunk * RET_CHUNK, RET_CHUNK), RET_CHUNK)

    def body(r, carry):
        ids = [r * span + j for j in range(span)]
        o_f = _ret_span(qf_ref, kf_ref, vf_ref, cf_ref, sf_ref, rf_ref, dm_f_ref[...], cr_f_ref[...],
                        kd_f_ref[...], cdec, ids)
        o_b = _ret_span(qb_ref, kb_ref, vb_ref, cb_ref, sb_ref, rb_ref, dm_b_ref[...], cr_b_ref[...],
                        kd_b_ref[...], cdec, [per_block - 1 - i for i in ids])
        for j in range(span):
            it = n * per_block + ids[j]
            o_ref[out_rows(it), :] += o_f[j]
            o_ref[out_rows(n_chunks - 1 - it), :] += o_b[j]
        return carry

    lax.fori_loop(0, per_block // span, body, 0)


def _retention_consts():
    idx = np.arange(RET_CHUNK, dtype=np.float64)
    lg = np.log(1.0 - 2.0 ** (-5.0 - np.arange(RET_HEADS, dtype=np.float64)))[:, None, None]
    rel = idx[:, None] - idx[None, :]
    dm_f = np.where(rel >= 0, np.exp(np.maximum(rel, 0.0) * lg), 0.0)
    dm_b = np.transpose(dm_f, (0, 2, 1))
    wide = np.ones((1, 1, RET_DK))
    cr_f = np.exp((idx + 1.0)[None, :, None] * lg) * wide
    kd_f = np.exp((RET_CHUNK - 1.0 - idx)[None, :, None] * lg) * wide
    cd = np.exp(RET_CHUNK * lg) * wide
    as32 = lambda a: jnp.asarray(a.astype(np.float32))
    return (as32(dm_f), as32(dm_b), as32(cr_f), as32(cr_f[:, ::-1]), as32(kd_f), as32(kd_f[:, ::-1]), as32(cd))


def _retention(proj, cos, sin):
    b, l, _ = proj.shape
    tt = min(l, TIME_BLOCK)
    nb = l // tt
    consts = _retention_consts()

    def tpos(n, bwd):
        return nb - 1 - n if bwd else n

    def col(off, bwd):
        base = off // RET_DK
        return pl.BlockSpec((None, tt, RET_DK), lambda i, h, n: (i, tpos(n, bwd), base + h))

    def tab(bwd):
        return pl.BlockSpec((None, tt, RET_DK // 2), lambda i, h, n: (i, tpos(n, bwd), 0))

    def per_head(shape):
        return pl.BlockSpec((None,) + shape, lambda i, h, n: (h, 0, 0))

    sq, wide, row = (RET_CHUNK, RET_CHUNK), (RET_CHUNK, RET_DK), (1, RET_DK)
    return pl.pallas_call(
        functools.partial(_ret_kernel, tt=tt),
        grid=(b, RET_HEADS, nb),
        in_specs=[per_head(sq), per_head(sq), per_head(wide), per_head(wide), per_head(wide), per_head(wide),
                  per_head(row),
                  col(OFF_RQ, False), col(OFF_RK, False), col(OFF_RV, False), tab(False), tab(False),
                  col(OFF_RQ, True), col(OFF_RK, True), col(OFF_RV, True), tab(True), tab(True)],
        out_specs=pl.BlockSpec((None, l, RET_DV), lambda i, h, n: (i, 0, h)),
        out_shape=jax.ShapeDtypeStruct((b, l, RET_HEADS * RET_DV), F32),
        scratch_shapes=[pltpu.VMEM((RET_DK, RET_DV), F32), pltpu.VMEM((RET_DK, RET_DV), F32)],
        compiler_params=pltpu.CompilerParams(dimension_semantics=("parallel", "parallel", "arbitrary"),
                                             vmem_limit_bytes=VMEM_LIMIT),
        name="retention",
    )(*consts, proj, proj, proj, cos, sin, proj, proj, proj, cos, sin)


def _norm_heads(o, n_heads):
    d = o.shape[1] // n_heads
    parts = []
    for h in range(n_heads):
        oh = o[:, h * d:(h + 1) * d]
        parts.append(oh * lax.rsqrt(jnp.mean(oh * oh, -1, keepdims=True) + RMS_EPS))
    return jnp.concatenate(parts, axis=1)


def _mix_kernel(ohg_ref, hgate_ref, oret_ref, rgate_ref, ga_ref, gb_ref, x_ref, ghg_ref, gret_ref,
                wbh_ref, wbr_ref, wo_ref, lng_ref, lnb_ref, wr_ref, x1_ref, logit_ref):
    a = _norm_heads(ohg_ref[...], HG_HEADS) * ghg_ref[...] * _silu(hgate_ref[...])
    y_hg = _dot(a.astype(BF16), wbh_ref[...])
    c = _norm_heads(oret_ref[...], RET_HEADS) * gret_ref[...] * _silu(rgate_ref[...])
    y_ret = _dot(c.astype(BF16), wbr_ref[...])
    merged = _sigmoid(ga_ref[...]) * y_hg + _sigmoid(gb_ref[...]) * y_ret
    mix = _dot(merged.astype(BF16), wo_ref[...])
    x1 = _layer_norm(DEEPNORM_ALPHA * x_ref[...] + mix, lng_ref[...], lnb_ref[...])
    x1_ref[...] = x1
    x_hi = x1.astype(BF16)
    x_lo = (x1 - x_hi.astype(F32)).astype(BF16)
    w = wr_ref[...]
    w_hi = w.astype(BF16)
    w_lo = (w - w_hi.astype(F32)).astype(BF16)
    by_hi = _dot(x_hi, jnp.concatenate([w_hi, w_lo], axis=1))
    logit_ref[...] = by_hi[:, :LANES] + by_hi[:, LANES:] + _dot(x_lo, w_hi)


def _mixer_out(o_hg, o_ret, proj, x, g_hg, g_ret, wbh, wbr, wo, ln_g, ln_b, w_route):
    t = x.shape[0]
    tm = min(t, MIX_TILE)
    d = D_MODEL
    rows = lambda cb: pl.BlockSpec((tm, d), lambda i: (i, cb))
    full = lambda shape: pl.BlockSpec(shape, lambda i: (0, 0))
    return pl.pallas_call(
        _mix_kernel,
        grid=(t // tm,),
        in_specs=[rows(0), rows(OFF_HGATE // d), rows(0), rows(OFF_RGATE // d), rows(OFF_GA // d),
                  rows(OFF_GB // d), rows(0), full((1, d)), full((1, d)),
                  full((d, d)), full((d, d)), full((d, d)), full((1, d)), full((1, d)), full((d, LANES))],
        out_specs=[pl.BlockSpec((tm, d), lambda i: (i, 0)), pl.BlockSpec((tm, LANES), lambda i: (i, 0))],
        out_shape=[jax.ShapeDtypeStruct((t, d), F32), jax.ShapeDtypeStruct((t, LANES), F32)],
        compiler_params=pltpu.CompilerParams(dimension_semantics=("parallel",), vmem_limit_bytes=VMEM_LIMIT),
        name="mixer_out",
    )(o_hg, proj, o_ret, proj, proj, proj, x, g_hg, g_ret, wbh, wbr, wo, ln_g, ln_b, w_route)


GROUP_LANE0 = N_EXPERTS


def _route_kernel(logit_ref, bias_ref, tri_ref, idx_ref, wt_ref, cnt_ref, carry_ref):
    i = pl.program_id(0)

    @pl.when(i == 0)
    def _():
        carry_ref[...] = jnp.zeros_like(carry_ref)

    lg = logit_ref[...] + bias_ref[...]
    tm = lg.shape[0]
    lane = lax.broadcasted_iota(jnp.int32, (tm, LANES), 1)
    first_of = lambda hit: jnp.min(jnp.where(hit, lane, LANES), -1, keepdims=True)

    g_mask = (lane >= GROUP_LANE0) & (lane < GROUP_LANE0 + N_GROUPS)
    g_l = jnp.where(g_mask, lg, NEG_BIG)
    g_max = jnp.max(g_l, -1, keepdims=True)
    grp = first_of(g_l == g_max) - GROUP_LANE0
    p_grp = 1.0 / jnp.sum(jnp.where(g_mask, jnp.exp(g_l - g_max), 0.0), -1, keepdims=True)

    e_lo = grp * EXPERTS_PER_GROUP
    e_l = jnp.where((lane >= e_lo) & (lane < e_lo + EXPERTS_PER_GROUP), lg, NEG_BIG)
    m1 = jnp.max(e_l, -1, keepdims=True)
    i1 = first_of(e_l == m1)
    e_l2 = jnp.where(lane == i1, NEG_BIG, e_l)
    m2 = jnp.max(e_l2, -1, keepdims=True)
    i2 = first_of(e_l2 == m2)
    r = jnp.exp(m2 - m1)
    w1 = p_grp / (1.0 + r)
    w2 = p_grp * r / (1.0 + r)

    hit1 = (lane == i1).astype(F32)
    hit2 = (lane == i2).astype(F32)
    both = hit1 + hit2
    before = _dot(tri_ref[...], both.astype(BF16)) + carry_ref[...]
    r1 = jnp.sum(hit1 * before, -1, keepdims=True).astype(jnp.int32)
    r2 = jnp.sum(hit2 * before, -1, keepdims=True).astype(jnp.int32)
    carry_ref[...] += jnp.sum(both, 0, keepdims=True)
    cnt_ref[...] = carry_ref[...]

    slot = lax.broadcasted_iota(jnp.int32, (tm, 8), 1)
    idx_ref[...] = jnp.where(slot == 0, i1, jnp.where(slot == 1, i2, jnp.where(slot == 2, r1, r2)))
    wt_ref[...] = jnp.where(slot == 0, w1, jnp.where(slot == 1, w2, 0.0))


def _route(logits, bias):
    t = logits.shape[0]
    tm = min(t, ROUTE_TILE)
    tri = jnp.asarray(np.tril(np.ones((tm, tm), np.float32), -1)).astype(BF16)
    return pl.pallas_call(
        _route_kernel,
        grid=(t // tm,),
        in_specs=[pl.BlockSpec((tm, LANES), lambda i: (i, 0)), pl.BlockSpec((1, LANES), lambda i: (0, 0)),
                  pl.BlockSpec((tm, tm), lambda i: (0, 0))],
        out_specs=[pl.BlockSpec((tm, 8), lambda i: (i, 0)), pl.BlockSpec((tm, 8), lambda i: (i, 0)),
                   pl.BlockSpec((1, LANES), lambda i: (0, 0))],
        out_shape=[jax.ShapeDtypeStruct((t, 8), jnp.int32), jax.ShapeDtypeStruct((t, 8), F32),
                   jax.ShapeDtypeStruct((1, LANES), F32)],
        scratch_shapes=[pltpu.VMEM((1, LANES), F32)],
        compiler_params=pltpu.CompilerParams(dimension_semantics=("arbitrary",)),
        name="route",
    )(logits, bias, tri)


def _row_copy(src, row, dst, dst_row, sem):
    return pltpu.make_async_copy(src.at[pl.ds(row, 1)], dst.at[pl.ds(dst_row, 1)], sem)


def _expert_kernel(be_ref, rows_ref, nused_ref, x_hbm, w1_ref, w3_ref, w2_ref, y_ref, buf, sem):
    i = pl.program_id(0)
    n_used = nused_ref[0]
    blk = buf.shape[1]

    def start(block, slot):
        def body(r, carry):
            _row_copy(x_hbm, rows_ref[block * blk + r], buf.at[slot], r, sem.at[slot]).start()
            return carry
        lax.fori_loop(0, blk, body, 0, unroll=DMA_UNROLL)

    @pl.when(i == 0)
    def _():
        start(0, 0)

    @pl.when(i + 1 < n_used)
    def _():
        start(i + 1, (i + 1) % 2)

    @pl.when(i < n_used)
    def _():
        slot = i % 2
        pltpu.make_async_copy(x_hbm.at[pl.ds(0, blk)], buf.at[slot], sem.at[slot]).wait()
        xs = buf[slot].astype(BF16)
        h = _silu(_dot(xs, w1_ref[...])) * _dot(xs, w3_ref[...])
        y_ref[...] = _dot(h.astype(BF16), w2_ref[...])

    @pl.when(i >= n_used)
    def _():
        y_ref[...] = jnp.zeros_like(y_ref)


def _experts(x1, block_expert, rows, n_used, w1, w3, w2, blk):
    n_blocks = block_expert.shape[0]
    d, de = D_MODEL, D_EXPERT
    grid_spec = pltpu.PrefetchScalarGridSpec(
        num_scalar_prefetch=3,
        grid=(n_blocks,),
        in_specs=[pl.BlockSpec(memory_space=pl.ANY),
                  pl.BlockSpec((None, d, de), lambda i, be, rw, nu: (be[i], 0, 0)),
                  pl.BlockSpec((None, d, de), lambda i, be, rw, nu: (be[i], 0, 0)),
                  pl.BlockSpec((None, de, d), lambda i, be, rw, nu: (be[i], 0, 0))],
        out_specs=pl.BlockSpec((blk, d), lambda i, be, rw, nu: (i, 0)),
        scratch_shapes=[pltpu.VMEM((2, blk, d), F32), pltpu.SemaphoreType.DMA((2,))],
    )
    return pl.pallas_call(
        _expert_kernel,
        grid_spec=grid_spec,
        out_shape=jax.ShapeDtypeStruct((n_blocks * blk, d), F32),
        compiler_params=pltpu.CompilerParams(dimension_semantics=("arbitrary",), vmem_limit_bytes=VMEM_LIMIT),
        name="experts",
    )(block_expert, rows, n_used, x1, w1, w3, w2)


def _combine_kernel(dest_ref, y_hbm, x1_ref, wt_ref, lng_ref, lnb_ref, o_ref, buf, sem):
    i = pl.program_id(0)
    n = pl.num_programs(0)
    tm = x1_ref.shape[0]

    def start(tile, slot):
        def body(r, carry):
            for k in range(TOP_K):
                _row_copy(y_hbm, dest_ref[(tile * tm + r) * TOP_K + k], buf.at[slot, k], r, sem.at[slot]).start()
            return carry
        lax.fori_loop(0, tm, body, 0, unroll=DMA_UNROLL)

    def drain(slot):
        for k in range(TOP_K):
            pltpu.make_async_copy(y_hbm.at[pl.ds(0, tm)], buf.at[slot, k], sem.at[slot]).wait()

    @pl.when(i == 0)
    def _():
        start(0, 0)

    @pl.when(i + 1 < n)
    def _():
        start(i + 1, (i + 1) % 2)

    slot = i % 2
    drain(slot)
    wt = wt_ref[...]
    moe = wt[:, 0:1] * buf[slot, 0] + wt[:, 1:2] * buf[slot, 1]
    o_ref[...] = _layer_norm(DEEPNORM_ALPHA * x1_ref[...] + moe, lng_ref[...], lnb_ref[...])


def _combine(dest, y, x1, wts, ln_g, ln_b):
    t, d = x1.shape
    tm = min(t, ROUTE_TILE)
    grid_spec = pltpu.PrefetchScalarGridSpec(
        num_scalar_prefetch=1,
        grid=(t // tm,),
        in_specs=[pl.BlockSpec(memory_space=pl.ANY),
                  pl.BlockSpec((tm, d), lambda i, ds: (i, 0)),
                  pl.BlockSpec((tm, 8), lambda i, ds: (i, 0)),
                  pl.BlockSpec((1, d), lambda i, ds: (0, 0)),
                  pl.BlockSpec((1, d), lambda i, ds: (0, 0))],
        out_specs=pl.BlockSpec((tm, d), lambda i, ds: (i, 0)),
        scratch_shapes=[pltpu.VMEM((2, TOP_K, tm, d), F32), pltpu.SemaphoreType.DMA((2,))],
    )
    return pl.pallas_call(
        _combine_kernel,
        grid_spec=grid_spec,
        out_shape=jax.ShapeDtypeStruct((t, d), F32),
        compiler_params=pltpu.CompilerParams(dimension_semantics=("arbitrary",), vmem_limit_bytes=VMEM_LIMIT),
        name="combine",
    )(dest, y, x1, wts, ln_g, ln_b)


def _deinterleave_cols(w):
    dk = RET_DK
    wh = w.reshape(w.shape[0], RET_HEADS, dk // 2, 2)
    return jnp.concatenate([wh[..., 0], wh[..., 1]], axis=-1).reshape(w.shape[0], RET_HEADS * dk)


def kernel(x, positions, w_in, hg_lb_logits, hg_norm_g, ret_norm_g, w_branch_hg, w_branch_ret, w_out,
           ln1_g, ln1_b, w_group, b_group, w_router, b_router, w1, w3, w2, ln2_g, ln2_b):
    b, l, d = x.shape
    t = b * l
    lb_cum = jnp.cumsum(jax.nn.softmax(hg_lb_logits.astype(F32), axis=0), axis=0)
    cos, sin = _rope_tables(positions)
    xf = x.reshape(t, d)
    for layer in range(DEPTH):
        lb = (lb_cum[layer + 1] - lb_cum[0]).reshape(1, -1)
        w = w_in[layer]
        w = jnp.concatenate([w[:, :OFF_RQ], _deinterleave_cols(w[:, OFF_RQ:OFF_RK]),
                             _deinterleave_cols(w[:, OFF_RK:OFF_RV]), w[:, OFF_RV:]], axis=1)
        proj = _matmul(xf.astype(BF16), w.astype(BF16), F32, 1024, 1024, "in_proj")
        proj3 = proj.reshape(b, l, PROJ_DIM)
        o_hg = _hgrn2(proj3, lb).reshape(t, d)
        o_ret = _retention(proj3, cos, sin).reshape(t, d)

        w_route = jnp.zeros((d, LANES), F32)
        w_route = w_route.at[:, :N_EXPERTS].set(w_router[layer]).at[:, GROUP_LANE0:GROUP_LANE0 + N_GROUPS].set(w_group[layer])
        b_route = jnp.zeros((1, LANES), F32)
        b_route = b_route.at[0, :N_EXPERTS].set(b_router[layer]).at[0, GROUP_LANE0:GROUP_LANE0 + N_GROUPS].set(b_group[layer])
        row = lambda v: v.reshape(1, -1)
        x1, logits = _mixer_out(o_hg, o_ret, proj, xf, row(hg_norm_g[layer]), row(ret_norm_g[layer]),
                                w_branch_hg[layer].astype(BF16), w_branch_ret[layer].astype(BF16),
                                w_out[layer].astype(BF16), row(ln1_g[layer]), row(ln1_b[layer]), w_route)

        idx, wts, counts = _route(logits, b_route)
        blk = MOE_BLOCK
        counts = counts[0, :N_EXPERTS].astype(jnp.int32)
        padded = (counts + blk - 1) // blk * blk
        pend = jnp.cumsum(padded)
        pstart = pend - padded
        dest = pstart[idx[:, :TOP_K]] + idx[:, TOP_K:2 * TOP_K]
        n_blocks = t * TOP_K // blk + N_EXPERTS
        block_row0 = jnp.arange(n_blocks, dtype=jnp.int32) * blk
        block_expert = jnp.minimum(jnp.sum((pend[None, :] <= block_row0[:, None]).astype(jnp.int32), axis=1),
                                   N_EXPERTS - 1)
        n_used = (pend[-1:] // blk).astype(jnp.int32)
        dest = dest.reshape(-1).astype(jnp.int32)
        tok = jnp.arange(t * TOP_K, dtype=jnp.int32) // TOP_K
        rows = jnp.zeros((n_blocks * blk,), jnp.int32).at[dest].set(tok)
        y = _experts(x1, block_expert, rows, n_used, w1[layer].astype(BF16), w3[layer].astype(BF16),
                     w2[layer].astype(BF16), blk)
        xf = _combine(dest, y, x1, wts, row(ln2_g[layer]), row(ln2_b[layer]))
    return xf.reshape(b, l, d)
```

```python
import functools

import numpy as np
import jax
import jax.numpy as jnp
from jax import lax
from jax.experimental import pallas as pl
from jax.experimental.pallas import tpu as pltpu

D_MODEL = 1024
HG_HEADS = 8
HG_DK = 128
HG_DV = D_MODEL // HG_HEADS
RET_HEADS = 4
RET_DK = 256
RET_DV = 256
ROPE_BASE = 10000.0
OFF_HQ = 0
OFF_HF_FWD = 1024
OFF_HF_BWD = 2048
OFF_HI = 3072
OFF_HGATE = 4096
OFF_RQ = 5120
OFF_RK = 6144
OFF_RV = 7168
OFF_RGATE = 8192
OFF_GA = 9216
OFF_GB = 10240
PROJ_DIM = 11264
N_GROUPS = 4
EXPERTS_PER_GROUP = 8
N_EXPERTS = N_GROUPS * EXPERTS_PER_GROUP
TOP_K = 2
D_EXPERT = 512
DEPTH = 1
DEEPNORM_ALPHA = (2 * DEPTH) ** 0.25
LN_EPS = 1e-5
RMS_EPS = 1e-6

LANES = 128
CHUNK = 64
RET_CHUNK = 128
RET_SPAN = 4
SUB = 16
TIME_BLOCK = 1024
MOE_BLOCK = 256
ROUTE_TILE = 256
DMA_UNROLL = 16
MIX_TILE = 256
VMEM_LIMIT = 56 * 1024 * 1024
NEG_BIG = -1e30
FAST_MIN_GATE = float(np.exp(-60.0 / SUB))
GROW_CAP = 64.0
FAST_SPAN = 16

F32 = jnp.float32
BF16 = jnp.bfloat16
_NT = (((1,), (1,)), ((), ()))
_TN = (((0,), (0,)), ((), ()))


def _sigmoid(v):
    return 1.0 / (1.0 + jnp.exp(-v))


def _silu(v):
    return v * _sigmoid(v)


def _dot(a, b, dims=None):
    if dims is None:
        return jnp.dot(a, b, preferred_element_type=F32)
    return lax.dot_general(a, b, dims, preferred_element_type=F32)


def _layer_norm(z, g, b):
    mu = jnp.mean(z, -1, keepdims=True)
    zc = z - mu
    var = jnp.mean(zc * zc, -1, keepdims=True)
    return zc * lax.rsqrt(var + LN_EPS) * g + b


def _rope_kernel(pos_ref, inv_ref, cos_ref, sin_ref):
    ang = pos_ref[...] * inv_ref[...]
    cos_ref[...] = jnp.cos(ang)
    sin_ref[...] = jnp.sin(ang)


def _rope_tables(positions):
    b, l = positions.shape
    tt = min(l, 1024)
    half = RET_DK // 2
    inv = (1.0 / (ROPE_BASE ** jnp.linspace(0.0, 1.0, half, dtype=F32))).reshape(1, half)
    pos = positions.astype(F32).reshape(b, l, 1)
    spec = pl.BlockSpec((None, tt, half), lambda i, j: (i, j, 0))
    return pl.pallas_call(
        _rope_kernel,
        grid=(b, l // tt),
        in_specs=[pl.BlockSpec((None, tt, 1), lambda i, j: (i, j, 0)),
                  pl.BlockSpec((1, half), lambda i, j: (0, 0))],
        out_specs=[spec, spec],
        out_shape=[jax.ShapeDtypeStruct((b, l, half), F32)] * 2,
        compiler_params=pltpu.CompilerParams(dimension_semantics=("parallel", "parallel")),
        name="rope_tables",
    )(pos, inv)


def _matmul_kernel(x_ref, w_ref, o_ref):
    o_ref[...] = _dot(x_ref[...], w_ref[...]).astype(o_ref.dtype)


def _matmul(x, w, out_dtype, tm, tn, name):
    m, k = x.shape
    n = w.shape[1]
    tm, tn = min(tm, m), min(tn, n)
    return pl.pallas_call(
        _matmul_kernel,
        grid=(n // tn, m // tm),
        in_specs=[pl.BlockSpec((tm, k), lambda j, i: (i, 0)),
                  pl.BlockSpec((k, tn), lambda j, i: (0, j))],
        out_specs=pl.BlockSpec((tm, tn), lambda j, i: (i, j)),
        out_shape=jax.ShapeDtypeStruct((m, n), out_dtype),
        compiler_params=pltpu.CompilerParams(dimension_semantics=("parallel", "parallel"),
                                             vmem_limit_bytes=VMEM_LIMIT),
        name=name,
    )(x, w)


def _hgrn_chunk(hq, hf, v, lb, s_t, tri, rev):
    c = hq.shape[0]
    q = _silu(hq)
    f = lb + (1.0 - lb) * _sigmoid(hf)
    k = 1.0 - f
    lf = jnp.log(f)
    g = jnp.dot(tri, lf, precision=lax.Precision.HIGHEST, preferred_element_type=F32)
    g_excl = g - lf
    g_tot = g[0:1] if rev else g[c - 1:c]

    vb = v.astype(BF16)
    o_inter = _dot((q * jnp.exp(g)).astype(BF16), s_t.astype(BF16), _NT)
    k_dec = (k * jnp.exp(g_tot - g)).astype(BF16)
    s_new = jnp.exp(g_tot) * s_t + _dot(vb, k_dec, _TN)

    ones = jnp.ones((LANES, LANES), BF16)
    t_idx = lax.broadcasted_iota(jnp.int32, (SUB, LANES), 0)
    outs = []
    for i in range(c // SUB):
        r0 = i * SUB
        q_i, k_i, g_i, v_i = q[r0:r0 + SUB], k[r0:r0 + SUB], g[r0:r0 + SUB], v[r0:r0 + SUB]
        lo, hi = (r0 + SUB, c) if rev else (0, r0)
        acc = jnp.zeros((SUB, v.shape[1]), F32)
        if hi > lo:
            first = r0 + SUB - 1 if rev else r0
            g_ref = g_excl[first:first + 1]
            q_t = (q_i * jnp.exp(g_i - g_ref)).astype(BF16)
            k_t = (k[lo:hi] * jnp.exp(g_ref - g[lo:hi])).astype(BF16)
            a = _dot(q_t, k_t, _NT)
            acc = _dot(a.astype(BF16), vb[lo:hi])
        prods = []
        for j in range(SUB):
            valid = (t_idx <= j) if rev else (t_idx >= j)
            e = jnp.exp(jnp.where(valid, g_i - g_i[j:j + 1], NEG_BIG))
            prods.append(q_i * k_i[j:j + 1] * e)
        a_rep = _dot(jnp.concatenate(prods, axis=0).astype(BF16), ones)
        for j in range(SUB):
            acc = acc + a_rep[j * SUB:(j + 1) * SUB] * v_i[j:j + 1]
        outs.append(acc)
    return o_inter + jnp.concatenate(outs, axis=0), s_new


def _hgrn_fast_span(lb, tri, allowed, q_ref, f_ref, v_ref, s_ref, chunk_ids, rev):
    c = CHUNK
    tri_b16 = tri.astype(BF16)
    prepped = []
    for cid in chunk_ids:
        rows = pl.ds(pl.multiple_of(cid * c, c), c)
        q = _silu(q_ref[rows, :])
        f = lb + (1.0 - lb) * _sigmoid(f_ref[rows, :])
        lf = jnp.log(f)
        hi = lf.astype(BF16)
        lo = (lf - hi.astype(F32)).astype(BF16)
        g2 = _dot(tri_b16, jnp.concatenate([hi, lo], axis=1))
        g = g2[:, :LANES] + g2[:, LANES:]
        prepped.append((q, 1.0 - f, lf, g, v_ref[rows, :].astype(BF16)))

    staged = []
    for q, k, lf, g, vb in prepped:
        eg = jnp.exp(g)
        total = eg[0:1] if rev else eg[c - 1:c]
        g_tot = g[0:1] if rev else g[c - 1:c]
        q_inter = (q * eg).astype(BF16)
        d_s = _dot(vb, (k * jnp.exp(g_tot - g)).astype(BF16), _TN)
        g_excl = g - lf
        scores = []
        for i in range(c // SUB):
            r0 = i * SUB
            first = r0 + SUB - 1 if rev else r0
            g_ref = g_excl[first:first + 1]
            q_t = (q[r0:r0 + SUB] * jnp.exp(g[r0:r0 + SUB] - g_ref)).astype(BF16)
            k_t = (k * jnp.exp(jnp.minimum(g_ref - g, GROW_CAP))).astype(BF16)
            scores.append(_dot(q_t, k_t, _NT))
        staged.append((q_inter, d_s, total, jnp.concatenate(scores, axis=0), vb))

    intra = [_dot((sc * allowed).astype(BF16), vb) for _, _, _, sc, vb in staged]

    outs = []
    s_t = s_ref[...]
    for (q_inter, d_s, total, _, _), o_intra in zip(staged, intra):
        outs.append(_dot(q_inter, s_t.astype(BF16), _NT) + o_intra)
        s_t = total * s_t + d_s
    s_ref[...] = s_t
    return outs


def _hgrn_kernel(lb_ref, tri_f_ref, tri_b_ref, qf_ref, ff_ref, vf_ref, qb_ref, fb_ref, vb_ref,
                 o_ref, sf_ref, sb_ref, *, tt):
    n = pl.program_id(2)
    per_block = tt // CHUNK
    n_chunks = pl.num_programs(2) * per_block

    @pl.when(n == 0)
    def _():
        sf_ref[...] = jnp.zeros_like(sf_ref)
        sb_ref[...] = jnp.zeros_like(sb_ref)
        o_ref[...] = jnp.zeros_like(o_ref)

    lb = lb_ref[...]

    def smallest_gate(f_ref):
        return jnp.min(lb + (1.0 - lb) * _sigmoid(jnp.min(f_ref[...], axis=0, keepdims=True)))

    fast_ok = jnp.minimum(smallest_gate(ff_ref), smallest_gate(fb_ref)) > FAST_MIN_GATE

    def out_rows(chunk):
        return pl.ds(pl.multiple_of(chunk * CHUNK, CHUNK), CHUNK)

    @pl.when(fast_ok)
    def _():
        span = min(FAST_SPAN, per_block)

        def body(r, carry):
            ids = [r * span + j for j in range(span)]
            o_f = _hgrn_fast_span(lb, tri_f_ref[...], tri_f_ref[...], qf_ref, ff_ref, vf_ref, sf_ref, ids, False)
            o_b = _hgrn_fast_span(lb, tri_b_ref[...], tri_b_ref[...], qb_ref, fb_ref, vb_ref, sb_ref,
                                  [per_block - 1 - i for i in ids], True)
            for j in range(span):
                it = n * per_block + ids[j]
                o_ref[out_rows(it), :] += o_f[j]
                o_ref[out_rows(n_chunks - 1 - it), :] += o_b[j]
            return carry

        lax.fori_loop(0, per_block // span, body, 0)

    @pl.when(jnp.logical_not(fast_ok))
    def _():
        def body(c, carry):
            it = n * per_block + c
            o_f, s_f = _hgrn_chunk(qf_ref[out_rows(c), :], ff_ref[out_rows(c), :], vf_ref[out_rows(c), :], lb,
                                   sf_ref[...], tri_f_ref[...], False)
            sf_ref[...] = s_f
            o_ref[out_rows(it), :] += o_f
            cb = per_block - 1 - c
            o_b, s_b = _hgrn_chunk(qb_ref[out_rows(cb), :], fb_ref[out_rows(cb), :], vb_ref[out_rows(cb), :], lb,
                                   sb_ref[...], tri_b_ref[...], True)
            sb_ref[...] = s_b
            o_ref[out_rows(n_chunks - 1 - it), :] += o_b
            return carry

        lax.fori_loop(0, per_block, body, 0)


def _hgrn2(proj, lb):
    b, l, _ = proj.shape
    tt = min(l, TIME_BLOCK)
    nb = l // tt
    ones = np.ones((CHUNK, CHUNK), np.float32)
    tri_f, tri_b = jnp.asarray(np.tril(ones)), jnp.asarray(np.triu(ones))

    def col(off, bwd):
        base = off // LANES
        if bwd:
            return pl.BlockSpec((None, tt, LANES), lambda i, h, n: (i, nb - 1 - n, base + h))
        return pl.BlockSpec((None, tt, LANES), lambda i, h, n: (i, n, base + h))

    sq_spec = pl.BlockSpec((CHUNK, CHUNK), lambda i, h, n: (0, 0))
    return pl.pallas_call(
        functools.partial(_hgrn_kernel, tt=tt),
        grid=(b, HG_HEADS, nb),
        in_specs=[pl.BlockSpec((1, LANES), lambda i, h, n: (0, h)), sq_spec, sq_spec,
                  col(OFF_HQ, False), col(OFF_HF_FWD, False), col(OFF_HI, False),
                  col(OFF_HQ, True), col(OFF_HF_BWD, True), col(OFF_HI, True)],
        out_specs=pl.BlockSpec((None, l, HG_DV), lambda i, h, n: (i, 0, h)),
        out_shape=jax.ShapeDtypeStruct((b, l, HG_HEADS * HG_DV), F32),
        scratch_shapes=[pltpu.VMEM((HG_DV, HG_DK), F32), pltpu.VMEM((HG_DV, HG_DK), F32)],
        compiler_params=pltpu.CompilerParams(dimension_semantics=("parallel", "parallel", "arbitrary"),
                                             vmem_limit_bytes=VMEM_LIMIT),
        name="hgrn2",
    )(lb, tri_f, tri_b, proj, proj, proj, proj, proj, proj)


def _rotate(t, cos, sin):
    half = t.shape[1] // 2
    t1, t2 = t[:, :half], t[:, half:]
    return jnp.concatenate([t1 * cos - t2 * sin, t1 * sin + t2 * cos], axis=1)


def _ret_span(q_ref, k_ref, v_ref, cos_ref, sin_ref, r_ref, dmat, cross, kdec, cdec, chunk_ids):
    c = RET_CHUNK
    staged = []
    for cid in chunk_ids:
        rows = pl.ds(pl.multiple_of(cid * c, c), c)
        cos, sin = cos_ref[rows, :], sin_ref[rows, :]
        qr = _rotate(q_ref[rows, :], cos, sin).astype(BF16)
        kr = _rotate(k_ref[rows, :], cos, sin) * (RET_DK ** -0.5)
        vb = v_ref[rows, :].astype(BF16)
        inner = _dot(qr, kr.astype(BF16), _NT)
        d_r = _dot((kr * kdec).astype(BF16), vb, _TN)
        staged.append((qr, vb, inner, d_r))
    intra = [_dot((inner * dmat).astype(BF16), vb) for _, vb, inner, _ in staged]
    outs = []
    r = r_ref[...]
    for (qr, _, _, d_r), o_intra in zip(staged, intra):
        outs.append(o_intra + cross * _dot(qr, r.astype(BF16)))
        r = cdec * r + d_r
    r_ref[...] = r
    return outs


def _ret_kernel(dm_f_ref, dm_b_ref, cr_f_ref, cr_b_ref, kd_f_ref, kd_b_ref, cd_ref,
                qf_ref, kf_ref, vf_ref, cf_ref, sf_ref, qb_ref, kb_ref, vb_ref, cb_ref, sb_ref,
                o_ref, rf_ref, rb_ref, *, tt):
    n = pl.program_id(2)
    per_block = tt // RET_CHUNK
    n_chunks = pl.num_programs(2) * per_block
    span = min(RET_SPAN, per_block)

    @pl.when(n == 0)
    def _():
        rf_ref[...] = jnp.zeros_like(rf_ref)
        rb_ref[...] = jnp.zeros_like(rb_ref)
        o_ref[...] = jnp.zeros_like(o_ref)

    cdec = cd_ref[...]

    def out_rows(chunk):
        return pl.ds(pl.multiple_of(chunk * RET_CHUNK, RET_CHUNK), RET_CHUNK)

    def body(r, carry):
        ids = [r * span + j for j in range(span)]
        o_f = _ret_span(qf_ref, kf_ref, vf_ref, cf_ref, sf_ref, rf_ref, dm_f_ref[...], cr_f_ref[...],
                        kd_f_ref[...], cdec, ids)
        o_b = _ret_span(qb_ref, kb_ref, vb_ref, cb_ref, sb_ref, rb_ref, dm_b_ref[...], cr_b_ref[...],
                        kd_b_ref[...], cdec, [per_block - 1 - i for i in ids])
        for j in range(span):
            it = n * per_block + ids[j]
            o_ref[out_rows(it), :] += o_f[j]
            o_ref[out_rows(n_chunks - 1 - it), :] += o_b[j]
        return carry

    lax.fori_loop(0, per_block // span, body, 0)


def _retention_consts():
    idx = np.arange(RET_CHUNK, dtype=np.float64)
    lg = np.log(1.0 - 2.0 ** (-5.0 - np.arange(RET_HEADS, dtype=np.float64)))[:, None, None]
    rel = idx[:, None] - idx[None, :]
    dm_f = np.where(rel >= 0, np.exp(np.maximum(rel, 0.0) * lg), 0.0)
    dm_b = np.transpose(dm_f, (0, 2, 1))
    wide = np.ones((1, 1, RET_DK))
    cr_f = np.exp((idx + 1.0)[None, :, None] * lg) * wide
    kd_f = np.exp((RET_CHUNK - 1.0 - idx)[None, :, None] * lg) * wide
    cd = np.exp(RET_CHUNK * lg) * wide
    as32 = lambda a: jnp.asarray(a.astype(np.float32))
    return (as32(dm_f), as32(dm_b), as32(cr_f), as32(cr_f[:, ::-1]), as32(kd_f), as32(kd_f[:, ::-1]), as32(cd))


def _retention(proj, cos, sin):
    b, l, _ = proj.shape
    tt = min(l, TIME_BLOCK)
    nb = l // tt
    consts = _retention_consts()

    def tpos(n, bwd):
        return nb - 1 - n if bwd else n

    def col(off, bwd):
        base = off // RET_DK
        return pl.BlockSpec((None, tt, RET_DK), lambda i, h, n: (i, tpos(n, bwd), base + h))

    def tab(bwd):
        return pl.BlockSpec((None, tt, RET_DK // 2), lambda i, h, n: (i, tpos(n, bwd), 0))

    def per_head(shape):
        return pl.BlockSpec((None,) + shape, lambda i, h, n: (h, 0, 0))

    sq, wide, row = (RET_CHUNK, RET_CHUNK), (RET_CHUNK, RET_DK), (1, RET_DK)
    return pl.pallas_call(
        functools.partial(_ret_kernel, tt=tt),
        grid=(b, RET_HEADS, nb),
        in_specs=[per_head(sq), per_head(sq), per_head(wide), per_head(wide), per_head(wide), per_head(wide),
                  per_head(row),
                  col(OFF_RQ, False), col(OFF_RK, False), col(OFF_RV, False), tab(False), tab(False),
                  col(OFF_RQ, True), col(OFF_RK, True), col(OFF_RV, True), tab(True), tab(True)],
        out_specs=pl.BlockSpec((None, l, RET_DV), lambda i, h, n: (i, 0, h)),
        out_shape=jax.ShapeDtypeStruct((b, l, RET_HEADS * RET_DV), F32),
        scratch_shapes=[pltpu.VMEM((RET_DK, RET_DV), F32), pltpu.VMEM((RET_DK, RET_DV), F32)],
        compiler_params=pltpu.CompilerParams(dimension_semantics=("parallel", "parallel", "arbitrary"),
                                             vmem_limit_bytes=VMEM_LIMIT),
        name="retention",
    )(*consts, proj, proj, proj, cos, sin, proj, proj, proj, cos, sin)


def _norm_heads(o, n_heads):
    d = o.shape[1] // n_heads
    parts = []
    for h in range(n_heads):
        oh = o[:, h * d:(h + 1) * d]
        parts.append(oh * lax.rsqrt(jnp.mean(oh * oh, -1, keepdims=True) + RMS_EPS))
    return jnp.concatenate(parts, axis=1)


def _mix_kernel(ohg_ref, hgate_ref, oret_ref, rgate_ref, ga_ref, gb_ref, x_ref, ghg_ref, gret_ref,
                wbh_ref, wbr_ref, wo_ref, lng_ref, lnb_ref, wr_ref, x1_ref, logit_ref):
    a = _norm_heads(ohg_ref[...], HG_HEADS) * ghg_ref[...] * _silu(hgate_ref[...])
    y_hg = _dot(a.astype(BF16), wbh_ref[...])
    c = _norm_heads(oret_ref[...], RET_HEADS) * gret_ref[...] * _silu(rgate_ref[...])
    y_ret = _dot(c.astype(BF16), wbr_ref[...])
    merged = _sigmoid(ga_ref[...]) * y_hg + _sigmoid(gb_ref[...]) * y_ret
    mix = _dot(merged.astype(BF16), wo_ref[...])
    x1 = _layer_norm(DEEPNORM_ALPHA * x_ref[...] + mix, lng_ref[...], lnb_ref[...])
    x1_ref[...] = x1
    x_hi = x1.astype(BF16)
    x_lo = (x1 - x_hi.astype(F32)).astype(BF16)
    w = wr_ref[...]
    w_hi = w.astype(BF16)
    w_lo = (w - w_hi.astype(F32)).astype(BF16)
    by_hi = _dot(x_hi, jnp.concatenate([w_hi, w_lo], axis=1))
    logit_ref[...] = by_hi[:, :LANES] + by_hi[:, LANES:] + _dot(x_lo, w_hi)


def _mixer_out(o_hg, o_ret, proj, x, g_hg, g_ret, wbh, wbr, wo, ln_g, ln_b, w_route):
    t = x.shape[0]
    tm = min(t, MIX_TILE)
    d = D_MODEL
    rows = lambda cb: pl.BlockSpec((tm, d), lambda i: (i, cb))
    full = lambda shape: pl.BlockSpec(shape, lambda i: (0, 0))
    return pl.pallas_call(
        _mix_kernel,
        grid=(t // tm,),
        in_specs=[rows(0), rows(OFF_HGATE // d), rows(0), rows(OFF_RGATE // d), rows(OFF_GA // d),
                  rows(OFF_GB // d), rows(0), full((1, d)), full((1, d)),
                  full((d, d)), full((d, d)), full((d, d)), full((1, d)), full((1, d)), full((d, LANES))],
        out_specs=[pl.BlockSpec((tm, d), lambda i: (i, 0)), pl.BlockSpec((tm, LANES), lambda i: (i, 0))],
        out_shape=[jax.ShapeDtypeStruct((t, d), F32), jax.ShapeDtypeStruct((t, LANES), F32)],
        compiler_params=pltpu.CompilerParams(dimension_semantics=("parallel",), vmem_limit_bytes=VMEM_LIMIT),
        name="mixer_out",
    )(o_hg, proj, o_ret, proj, proj, proj, x, g_hg, g_ret, wbh, wbr, wo, ln_g, ln_b, w_route)


GROUP_LANE0 = N_EXPERTS


def _route_kernel(logit_ref, bias_ref, tri_ref, idx_ref, wt_ref, cnt_ref, carry_ref):
    i = pl.program_id(0)

    @pl.when(i == 0)
    def _():
        carry_ref[...] = jnp.zeros_like(carry_ref)

    lg = logit_ref[...] + bias_ref[...]
    tm = lg.shape[0]
    lane = lax.broadcasted_iota(jnp.int32, (tm, LANES), 1)
    first_of = lambda hit: jnp.min(jnp.where(hit, lane, LANES), -1, keepdims=True)

    g_mask = (lane >= GROUP_LANE0) & (lane < GROUP_LANE0 + N_GROUPS)
    g_l = jnp.where(g_mask, lg, NEG_BIG)
    g_max = jnp.max(g_l, -1, keepdims=True)
    grp = first_of(g_l == g_max) - GROUP_LANE0
    p_grp = 1.0 / jnp.sum(jnp.where(g_mask, jnp.exp(g_l - g_max), 0.0), -1, keepdims=True)

    e_lo = grp * EXPERTS_PER_GROUP
    e_l = jnp.where((lane >= e_lo) & (lane < e_lo + EXPERTS_PER_GROUP), lg, NEG_BIG)
    m1 = jnp.max(e_l, -1, keepdims=True)
    i1 = first_of(e_l == m1)
    e_l2 = jnp.where(lane == i1, NEG_BIG, e_l)
    m2 = jnp.max(e_l2, -1, keepdims=True)
    i2 = first_of(e_l2 == m2)
    r = jnp.exp(m2 - m1)
    w1 = p_grp / (1.0 + r)
    w2 = p_grp * r / (1.0 + r)

    hit1 = (lane == i1).astype(F32)
    hit2 = (lane == i2).astype(F32)
    both = hit1 + hit2
    before = _dot(tri_ref[...], both.astype(BF16)) + carry_ref[...]
    r1 = jnp.sum(hit1 * before, -1, keepdims=True).astype(jnp.int32)
    r2 = jnp.sum(hit2 * before, -1, keepdims=True).astype(jnp.int32)
    carry_ref[...] += jnp.sum(both, 0, keepdims=True)
    cnt_ref[...] = carry_ref[...]

    slot = lax.broadcasted_iota(jnp.int32, (tm, 8), 1)
    idx_ref[...] = jnp.where(slot == 0, i1, jnp.where(slot == 1, i2, jnp.where(slot == 2, r1, r2)))
    wt_ref[...] = jnp.where(slot == 0, w1, jnp.where(slot == 1, w2, 0.0))


def _route(logits, bias):
    t = logits.shape[0]
    tm = min(t, ROUTE_TILE)
    tri = jnp.asarray(np.tril(np.ones((tm, tm), np.float32), -1)).astype(BF16)
    return pl.pallas_call(
        _route_kernel,
        grid=(t // tm,),
        in_specs=[pl.BlockSpec((tm, LANES), lambda i: (i, 0)), pl.BlockSpec((1, LANES), lambda i: (0, 0)),
                  pl.BlockSpec((tm, tm), lambda i: (0, 0))],
        out_specs=[pl.BlockSpec((tm, 8), lambda i: (i, 0)), pl.BlockSpec((tm, 8), lambda i: (i, 0)),
                   pl.BlockSpec((1, LANES), lambda i: (0, 0))],
        out_shape=[jax.ShapeDtypeStruct((t, 8), jnp.int32), jax.ShapeDtypeStruct((t, 8), F32),
                   jax.ShapeDtypeStruct((1, LANES), F32)],
        scratch_shapes=[pltpu.VMEM((1, LANES), F32)],
        compiler_params=pltpu.CompilerParams(dimension_semantics=("arbitrary",)),
        name="route",
    )(logits, bias, tri)


def _row_copy(src, row, dst, dst_row, sem):
    return pltpu.make_async_copy(src.at[pl.ds(row, 1)], dst.at[pl.ds(dst_row, 1)], sem)


def _expert_kernel(be_ref, rows_ref, nused_ref, x_hbm, w1_ref, w3_ref, w2_ref, y_ref, buf, sem):
    i = pl.program_id(0)
    n_used = nused_ref[0]
    blk = buf.shape[1]

    def start(block, slot):
        def body(g, carry):
            for p in range(2):
                r = 2 * g + p
                _row_copy(x_hbm, rows_ref[block * blk + r], buf.at[slot], r, sem.at[slot]).start(priority=p)
            return carry
        lax.fori_loop(0, blk // 2, body, 0, unroll=DMA_UNROLL // 2)

    @pl.when(i == 0)
    def _():
        start(0, 0)

    @pl.when(i + 1 < n_used)
    def _():
        start(i + 1, (i + 1) % 2)

    @pl.when(i < n_used)
    def _():
        slot = i % 2
        pltpu.make_async_copy(x_hbm.at[pl.ds(0, blk)], buf.at[slot], sem.at[slot]).wait()
        xs = buf[slot].astype(BF16)
        h = _silu(_dot(xs, w1_ref[...])) * _dot(xs, w3_ref[...])
        y_ref[...] = _dot(h.astype(BF16), w2_ref[...])

    @pl.when(i >= n_used)
    def _():
        y_ref[...] = jnp.zeros_like(y_ref)


def _experts(x1, block_expert, rows, n_used, w1, w3, w2, blk):
    n_blocks = block_expert.shape[0]
    d, de = D_MODEL, D_EXPERT
    grid_spec = pltpu.PrefetchScalarGridSpec(
        num_scalar_prefetch=3,
        grid=(n_blocks,),
        in_specs=[pl.BlockSpec(memory_space=pl.ANY),
                  pl.BlockSpec((None, d, de), lambda i, be, rw, nu: (be[i], 0, 0)),
                  pl.BlockSpec((None, d, de), lambda i, be, rw, nu: (be[i], 0, 0)),
                  pl.BlockSpec((None, de, d), lambda i, be, rw, nu: (be[i], 0, 0))],
        out_specs=pl.BlockSpec((blk, d), lambda i, be, rw, nu: (i, 0)),
        scratch_shapes=[pltpu.VMEM((2, blk, d), F32), pltpu.SemaphoreType.DMA((2,))],
    )
    return pl.pallas_call(
        _expert_kernel,
        grid_spec=grid_spec,
        out_shape=jax.ShapeDtypeStruct((n_blocks * blk, d), F32),
        compiler_params=pltpu.CompilerParams(dimension_semantics=("arbitrary",), vmem_limit_bytes=VMEM_LIMIT),
        name="experts",
    )(block_expert, rows, n_used, x1, w1, w3, w2)


def _combine_kernel(dest_ref, y_hbm, x1_ref, wt_ref, lng_ref, lnb_ref, o_ref, buf, sem):
    i = pl.program_id(0)
    n = pl.num_programs(0)
    tm = x1_ref.shape[0]

    def start(tile, slot):
        def body(r, carry):
            for k in range(TOP_K):
                _row_copy(y_hbm, dest_ref[(tile * tm + r) * TOP_K + k], buf.at[slot, k], r,
                          sem.at[slot]).start(priority=k)
            return carry
        lax.fori_loop(0, tm, body, 0, unroll=DMA_UNROLL)

    def drain(slot):
        for k in range(TOP_K):
            pltpu.make_async_copy(y_hbm.at[pl.ds(0, tm)], buf.at[slot, k], sem.at[slot]).wait()

    @pl.when(i == 0)
    def _():
        start(0, 0)

    @pl.when(i + 1 < n)
    def _():
        start(i + 1, (i + 1) % 2)

    slot = i % 2
    drain(slot)
    wt = wt_ref[...]
    moe = wt[:, 0:1] * buf[slot, 0] + wt[:, 1:2] * buf[slot, 1]
    o_ref[...] = _layer_norm(DEEPNORM_ALPHA * x1_ref[...] + moe, lng_ref[...], lnb_ref[...])


def _combine(dest, y, x1, wts, ln_g, ln_b):
    t, d = x1.shape
    tm = min(t, ROUTE_TILE)
    grid_spec = pltpu.PrefetchScalarGridSpec(
        num_scalar_prefetch=1,
        grid=(t // tm,),
        in_specs=[pl.BlockSpec(memory_space=pl.ANY),
                  pl.BlockSpec((tm, d), lambda i, ds: (i, 0)),
                  pl.BlockSpec((tm, 8), lambda i, ds: (i, 0)),
                  pl.BlockSpec((1, d), lambda i, ds: (0, 0)),
                  pl.BlockSpec((1, d), lambda i, ds: (0, 0))],
        out_specs=pl.BlockSpec((tm, d), lambda i, ds: (i, 0)),
        scratch_shapes=[pltpu.VMEM((2, TOP_K, tm, d), F32), pltpu.SemaphoreType.DMA((2,))],
    )
    return pl.pallas_call(
        _combine_kernel,
        grid_spec=grid_spec,
        out_shape=jax.ShapeDtypeStruct((t, d), F32),
        compiler_params=pltpu.CompilerParams(dimension_semantics=("arbitrary",), vmem_limit_bytes=VMEM_LIMIT),
        name="combine",
    )(dest, y, x1, wts, ln_g, ln_b)


def _deinterleave_cols(w):
    dk = RET_DK
    wh = w.reshape(w.shape[0], RET_HEADS, dk // 2, 2)
    return jnp.concatenate([wh[..., 0], wh[..., 1]], axis=-1).reshape(w.shape[0], RET_HEADS * dk)


def kernel(x, positions, w_in, hg_lb_logits, hg_norm_g, ret_norm_g, w_branch_hg, w_branch_ret, w_out,
           ln1_g, ln1_b, w_group, b_group, w_router, b_router, w1, w3, w2, ln2_g, ln2_b):
    b, l, d = x.shape
    t = b * l
    lb_cum = jnp.cumsum(jax.nn.softmax(hg_lb_logits.astype(F32), axis=0), axis=0)
    cos, sin = _rope_tables(positions)
    xf = x.reshape(t, d)
    for layer in range(DEPTH):
        lb = (lb_cum[layer + 1] - lb_cum[0]).reshape(1, -1)
        w = w_in[layer]
        w = jnp.concatenate([w[:, :OFF_RQ], _deinterleave_cols(w[:, OFF_RQ:OFF_RK]),
                             _deinterleave_cols(w[:, OFF_RK:OFF_RV]), w[:, OFF_RV:]], axis=1)
        proj = _matmul(xf.astype(BF16), w.astype(BF16), F32, 1024, 1024, "in_proj")
        proj3 = proj.reshape(b, l, PROJ_DIM)
        o_hg = _hgrn2(proj3, lb).reshape(t, d)
        o_ret = _retention(proj3, cos, sin).reshape(t, d)

        w_route = jnp.zeros((d, LANES), F32)
        w_route = w_route.at[:, :N_EXPERTS].set(w_router[layer]).at[:, GROUP_LANE0:GROUP_LANE0 + N_GROUPS].set(w_group[layer])
        b_route = jnp.zeros((1, LANES), F32)
        b_route = b_route.at[0, :N_EXPERTS].set(b_router[layer]).at[0, GROUP_LANE0:GROUP_LANE0 + N_GROUPS].set(b_group[layer])
        row = lambda v: v.reshape(1, -1)
        x1, logits = _mixer_out(o_hg, o_ret, proj, xf, row(hg_norm_g[layer]), row(ret_norm_g[layer]),
                                w_branch_hg[layer].astype(BF16), w_branch_ret[layer].astype(BF16),
                                w_out[layer].astype(BF16), row(ln1_g[layer]), row(ln1_b[layer]), w_route)

        idx, wts, counts = _route(logits, b_route)
        blk = MOE_BLOCK
        counts = counts[0, :N_EXPERTS].astype(jnp.int32)
        padded = (counts + blk - 1) // blk * blk
        pend = jnp.cumsum(padded)
        pstart = pend - padded
        dest = pstart[idx[:, :TOP_K]] + idx[:, TOP_K:2 * TOP_K]
        n_blocks = t * TOP_K // blk + N_EXPERTS
        block_row0 = jnp.arange(n_blocks, dtype=jnp.int32) * blk
        block_expert = jnp.minimum(jnp.sum((pend[None, :] <= block_row0[:, None]).astype(jnp.int32), axis=1),
                                   N_EXPERTS - 1)
        n_used = (pend[-1:] // blk).astype(jnp.int32)
        dest = dest.reshape(-1).astype(jnp.int32)
        tok = jnp.arange(t * TOP_K, dtype=jnp.int32) // TOP_K
        rows = jnp.zeros((n_blocks * blk,), jnp.int32).at[dest].set(tok)
        y = _experts(x1, block_expert, rows, n_used, w1[layer].astype(BF16), w3[layer].astype(BF16),
                     w2[layer].astype(BF16), blk)
        xf = _combine(dest, y, x1, wts, row(ln2_g[layer]), row(ln2_b[layer]))
    return xf.reshape(b, l, d)
```
